```python
import math
import jax, jax.numpy as jnp
from jax import lax
import numpy as np

D_MODEL = 2048
BATCH = 1
SEQ = 8192
DEPTH = 1
DEC_BATCH = 128
DEC_SEQ = 1
PAST_LEN = 2048
PAGE_SIZE = 128

HEAD_DIM = 128
N_SB_HEADS = 8
N_DIFF_HEADS = 4
SB_WIDTH = N_SB_HEADS * HEAD_DIM
DIFF_WIDTH = N_DIFF_HEADS * 2 * HEAD_DIM
D_FF = 4 * D_MODEL
N_META = 16
Q_BLOCK = 128
ROPE_THETA = 500000.0
ROT_DIM = HEAD_DIM // 4
ALPHA = (2 * DEPTH) ** 0.25
BETA = (8 * DEPTH) ** -0.25
LN_EPS = 1e-5
IN_COLS = 3 * SB_WIDTH + 3 * DIFF_WIDTH + 2 * D_MODEL
SPLITS = (SB_WIDTH, 2 * SB_WIDTH, 3 * SB_WIDTH,
          3 * SB_WIDTH + DIFF_WIDTH, 3 * SB_WIDTH + 2 * DIFF_WIDTH, 3 * SB_WIDTH + 3 * DIFF_WIDTH,
          3 * SB_WIDTH + 3 * DIFF_WIDTH + D_MODEL)

kernel_name = 'stickbreak_diffattn_gated_hybrid_step'


def layer_norm(x, g, b):
    xf = x.astype(jnp.float32)
    mu = xf.mean(-1, keepdims=True)
    var = jnp.square(xf - mu).mean(-1, keepdims=True)
    return ((xf - mu) * lax.rsqrt(var + LN_EPS)).astype(x.dtype) * g + b


def partial_rope(x, pos):
    half = ROT_DIM // 2
    inv = ROPE_THETA ** (-jnp.arange(half, dtype=jnp.float32) / half)
    ang = pos.astype(jnp.float32)[:, None] * inv[None, :]
    cos = jnp.cos(ang)[:, None, None, :]
    sin = jnp.sin(ang)[:, None, None, :]
    xf = x.astype(jnp.float32)
    x1, x2, rest = xf[..., :half], xf[..., half:ROT_DIM], xf[..., ROT_DIM:]
    out = jnp.concatenate([x1 * cos - x2 * sin, x2 * cos + x1 * sin, rest], axis=-1)
    return out.astype(x.dtype)


def project(h, w_in, pos):
    b_, t_ = h.shape[:2]
    z = h @ w_in
    sbq, sbk, sbv, dq, dk, dv, zg_sb, zg_d = jnp.split(z, SPLITS, axis=-1)
    sbq = sbq.reshape(b_, t_, N_SB_HEADS, HEAD_DIM)
    sbk = sbk.reshape(b_, t_, N_SB_HEADS, HEAD_DIM)
    sbv = sbv.reshape(b_, t_, N_SB_HEADS, HEAD_DIM)
    dq = partial_rope(dq.reshape(b_, t_, N_DIFF_HEADS, 2, HEAD_DIM), pos)
    dk = partial_rope(dk.reshape(b_, t_, N_DIFF_HEADS, 2, HEAD_DIM), pos)
    dv = dv.reshape(b_, t_, N_DIFF_HEADS, 2 * HEAD_DIM)
    return sbq, sbk, sbv, dq, dk, dv, jax.nn.sigmoid(zg_sb), jax.nn.sigmoid(zg_d)


def attend_block(sb_q, d_q, qpos, sb_k, sb_v, d_k, d_v, kpos, lam):
    scale = HEAD_DIM ** -0.5
    z = jnp.einsum('bqhd,bkhd->bhqk', sb_q, sb_k).astype(jnp.float32) * scale
    strict = kpos[None, :] < qpos[:, None]
    log_keep = jnp.where(strict, jax.nn.log_sigmoid(-z), 0.0)
    suffix = lax.cumsum(log_keep, axis=3, reverse=True) - log_keep
    w_sb = jnp.where(strict, jnp.exp(jax.nn.log_sigmoid(z) + suffix), 0.0)
    o_sb = jnp.einsum('bhqk,bkhd->bqhd', w_sb.astype(sb_v.dtype), sb_v)
    s = jnp.einsum('bqhcd,bkhcd->bhcqk', d_q, d_k).astype(jnp.float32) * scale
    causal = kpos[None, :] <= qpos[:, None]
    p = jax.nn.softmax(jnp.where(causal, s, -jnp.inf), axis=-1)
    a = p[:, :, 0] - lam * p[:, :, 1]
    o_d = jnp.einsum('bhqk,bkhe->bqhe', a.astype(d_v.dtype), d_v)
    return o_sb, o_d


def merge_and_ffn(x, o_sb, o_d, g_sb, g_d, lam_init, diff_norm_g, w_branch_sb, w_branch_diff,
                  w_out, ln_mix_g, ln_mix_b, w_up, w_down, ln_ffn_g, ln_ffn_b):
    b_, t_ = x.shape[:2]
    of = o_d.astype(jnp.float32)
    of = of * lax.rsqrt(jnp.mean(jnp.square(of), axis=-1, keepdims=True) + LN_EPS)
    o_d = (of.astype(x.dtype) * diff_norm_g) * (1.0 - lam_init)
    br_sb = o_sb.reshape(b_, t_, SB_WIDTH) @ w_branch_sb
    br_d = o_d.reshape(b_, t_, DIFF_WIDTH) @ w_branch_diff
    mix = (g_sb * br_sb + g_d * br_d) @ w_out
    x = layer_norm(ALPHA * x + mix, ln_mix_g, ln_mix_b)
    h = jnp.square(jax.nn.relu(x @ w_up)) @ w_down
    return layer_norm(ALPHA * x + h, ln_ffn_g, ln_ffn_b)


def setup_inputs(seed: int = 0) -> dict:
    key = jax.random.key(seed)
    ks = jax.random.split(key, 24)
    f32 = jnp.float32
    n_pages = PAST_LEN // PAGE_SIZE
    n_pool = (DEC_BATCH * n_pages * 5 + 3) // 4
    col_scale = jnp.concatenate([
        jnp.ones((2 * SB_WIDTH,), f32), jnp.full((SB_WIDTH,), BETA, f32),
        jnp.ones((2 * DIFF_WIDTH,), f32), jnp.full((DIFF_WIDTH,), BETA, f32),
        jnp.ones((2 * D_MODEL,), f32)])
    perm = jax.random.permutation(ks[4], n_pool)[:DEC_BATCH * n_pages]
    return {
        'x_prompt': jax.random.normal(ks[0], (BATCH, SEQ, D_MODEL), f32),
        'x_sample': jax.random.normal(ks[1], (DEC_BATCH, DEC_SEQ, D_MODEL), f32),
        'cache_sb_kv': jax.random.normal(ks[2], (DEPTH, n_pool, PAGE_SIZE, 2, N_SB_HEADS, HEAD_DIM), f32),
        'cache_diff_kv': jax.random.normal(ks[3], (DEPTH, n_pool, PAGE_SIZE, 2, N_DIFF_HEADS, 2 * HEAD_DIM), f32),
        'page_table': perm.reshape(DEC_BATCH, n_pages).astype(jnp.int32),
        'meta_tokens': jax.random.normal(ks[5], (N_META, D_MODEL), f32),
        'w_in': jax.random.normal(ks[6], (DEPTH, D_MODEL, IN_COLS), f32) * (D_MODEL ** -0.5) * col_scale,
        'lambda_q1': 0.1 * jax.random.normal(ks[7], (DEPTH, HEAD_DIM), f32),
        'lambda_k1': 0.1 * jax.random.normal(ks[8], (DEPTH, HEAD_DIM), f32),
        'lambda_q2': 0.1 * jax.random.normal(ks[9], (DEPTH, HEAD_DIM), f32),
        'lambda_k2': 0.1 * jax.random.normal(ks[10], (DEPTH, HEAD_DIM), f32),
        'diff_norm_g': 1.0 + 0.02 * jax.random.normal(ks[11], (DEPTH, 2 * HEAD_DIM), f32),
        'w_branch_sb': jax.random.normal(ks[12], (DEPTH, SB_WIDTH, D_MODEL), f32) * (SB_WIDTH ** -0.5) * BETA,
        'w_branch_diff': jax.random.normal(ks[13], (DEPTH, DIFF_WIDTH, D_MODEL), f32) * (DIFF_WIDTH ** -0.5) * BETA,
        'w_out': jax.random.normal(ks[14], (DEPTH, D_MODEL, D_MODEL), f32) * (D_MODEL ** -0.5) * BETA,
        'ln_mix_g': 1.0 + 0.02 * jax.random.normal(ks[15], (DEPTH, D_MODEL), f32),
        'ln_mix_b': 0.02 * jax.random.normal(ks[16], (DEPTH, D_MODEL), f32),
        'w_up': jax.random.normal(ks[17], (DEPTH, D_MODEL, D_FF), f32) * (D_MODEL ** -0.5) * BETA,
        'w_down': jax.random.normal(ks[18], (DEPTH, D_FF, D_MODEL), f32) * (D_FF ** -0.5) * BETA,
        'ln_ffn_g': 1.0 + 0.02 * jax.random.normal(ks[19], (DEPTH, D_MODEL), f32),
        'ln_ffn_b': 0.02 * jax.random.normal(ks[20], (DEPTH, D_MODEL), f32),
    }


def reference(x_prompt, x_sample, cache_sb_kv, cache_diff_kv, page_table, meta_tokens, w_in,
              lambda_q1, lambda_k1, lambda_q2, lambda_k2, diff_norm_g, w_branch_sb, w_branch_diff,
              w_out, ln_mix_g, ln_mix_b, w_up, w_down, ln_ffn_g, ln_ffn_b):
    n_blocks = SEQ // Q_BLOCK
    t_p = N_META + SEQ
    xp = jnp.concatenate([jnp.broadcast_to(meta_tokens[None].astype(x_prompt.dtype),
                                           (BATCH, N_META, D_MODEL)), x_prompt], axis=1)
    xs = x_sample
    pos_p = jnp.arange(t_p)
    pos_s = PAST_LEN + jnp.arange(DEC_SEQ)
    kpos_s = jnp.arange(PAST_LEN + DEC_SEQ)
    sb_p_rows, d_p_rows, sb_s_rows, d_s_rows = [], [], [], []

    for l in range(DEPTH):
        lam_init = 0.8 - 0.6 * math.exp(-0.3 * l)
        lam = (jnp.exp(jnp.sum(lambda_q1[l].astype(jnp.float32) * lambda_k1[l].astype(jnp.float32)))
               - jnp.exp(jnp.sum(lambda_q2[l].astype(jnp.float32) * lambda_k2[l].astype(jnp.float32)))
               + lam_init)

        sbq, sbk, sbv, dq, dk, dv, g_sb, g_d = project(xp, w_in[l], pos_p)
        o_sb_m, o_d_m = attend_block(sbq[:, :N_META], dq[:, :N_META], pos_p[:N_META],
                                     sbk, sbv, dk, dv, pos_p, lam)
        sbq_b = sbq[:, N_META:].reshape(BATCH, n_blocks, Q_BLOCK, N_SB_HEADS, HEAD_DIM).swapaxes(0, 1)
        dq_b = dq[:, N_META:].reshape(BATCH, n_blocks, Q_BLOCK, N_DIFF_HEADS, 2, HEAD_DIM).swapaxes(0, 1)
        qpos_b = pos_p[N_META:].reshape(n_blocks, Q_BLOCK)
        o_sb_r, o_d_r = lax.map(
            lambda a: attend_block(a[0], a[1], a[2], sbk, sbv, dk, dv, pos_p, lam),
            (sbq_b, dq_b, qpos_b))
        o_sb_r = o_sb_r.swapaxes(0, 1).reshape(BATCH, SEQ, N_SB_HEADS, HEAD_DIM)
        o_d_r = o_d_r.swapaxes(0, 1).reshape(BATCH, SEQ, N_DIFF_HEADS, 2 * HEAD_DIM)
        o_sb = jnp.concatenate([o_sb_m, o_sb_r], axis=1)
        o_d = jnp.concatenate([o_d_m, o_d_r], axis=1)
        sb_p_rows.append(jnp.stack([sbk, sbv], axis=2))
        d_p_rows.append(jnp.stack([dk.reshape(BATCH, t_p, N_DIFF_HEADS, 2 * HEAD_DIM), dv], axis=2))
        xp = merge_and_ffn(xp, o_sb, o_d, g_sb, g_d, lam_init, diff_norm_g[l], w_branch_sb[l],
                           w_branch_diff[l], w_out[l], ln_mix_g[l], ln_mix_b[l], w_up[l], w_down[l],
                           ln_ffn_g[l], ln_ffn_b[l])

        sbq_s, sbk_s, sbv_s, dq_s, dk_s, dv_s, gs_sb, gs_d = project(xs, w_in[l], pos_s)
        sb_past = cache_sb_kv[l][page_table].reshape(DEC_BATCH, PAST_LEN, 2, N_SB_HEADS, HEAD_DIM)
        d_past = cache_diff_kv[l][page_table].reshape(DEC_BATCH, PAST_LEN, 2, N_DIFF_HEADS, 2 * HEAD_DIM)
        sbk_all = jnp.concatenate([sb_past[:, :, 0].astype(sbk_s.dtype), sbk_s], axis=1)
        sbv_all = jnp.concatenate([sb_past[:, :, 1].astype(sbv_s.dtype), sbv_s], axis=1)
        dk_all = jnp.concatenate([d_past[:, :, 0].reshape(DEC_BATCH, PAST_LEN, N_DIFF_HEADS, 2, HEAD_DIM)
                                  .astype(dk_s.dtype), dk_s], axis=1)
        dv_all = jnp.concatenate([d_past[:, :, 1].astype(dv_s.dtype), dv_s], axis=1)
        os_sb, os_d = attend_block(sbq_s, dq_s, pos_s, sbk_all, sbv_all, dk_all, dv_all, kpos_s, lam)
        sb_s_rows.append(jnp.stack([sbk_s, sbv_s], axis=2))
        d_s_rows.append(jnp.stack([dk_s.reshape(DEC_BATCH, DEC_SEQ, N_DIFF_HEADS, 2 * HEAD_DIM), dv_s], axis=2))
        xs = merge_and_ffn(xs, os_sb, os_d, gs_sb, gs_d, lam_init, diff_norm_g[l], w_branch_sb[l],
                           w_branch_diff[l], w_out[l], ln_mix_g[l], ln_mix_b[l], w_up[l], w_down[l],
                           ln_ffn_g[l], ln_ffn_b[l])

    y_prompt = xp[:, N_META:]
    y_sample = xs
    sb_kv_prompt = jnp.stack(sb_p_rows, axis=0)
    diff_kv_prompt = jnp.stack(d_p_rows, axis=0)
    sb_kv_sample = jnp.stack(sb_s_rows, axis=0)
    diff_kv_sample = jnp.stack(d_s_rows, axis=0)
    return (y_prompt, y_sample, sb_kv_prompt, diff_kv_prompt, sb_kv_sample, diff_kv_sample)
```

```python
import functools
import math

import jax
import jax.numpy as jnp
from jax import lax
from jax.experimental import pallas as pl
from jax.experimental.pallas import tpu as pltpu

F32 = jnp.float32
BF16 = jnp.bfloat16

HEAD_DIM = 128
N_SB_HEADS = 8
N_DIFF_HEADS = 4
SB_WIDTH = N_SB_HEADS * HEAD_DIM
DIFF_WIDTH = N_DIFF_HEADS * 2 * HEAD_DIM
N_META = 16
ROT_DIM = HEAD_DIM // 4
ROPE_THETA = 500000.0
LN_EPS = 1e-5
DEPTH = 1
ALPHA = (2 * DEPTH) ** 0.25
LAM_INIT = 0.8 - 0.6 * math.exp(-0.3 * 0)
QK_SCALE = HEAD_DIM ** -0.5
LANES = 128
VMEM_LIMIT = 56 * 1024 * 1024

COL_SBQ, COL_SBK, COL_SBV = 0, SB_WIDTH, 2 * SB_WIDTH
COL_DQ, COL_DK, COL_DV = 3 * SB_WIDTH, 3 * SB_WIDTH + DIFF_WIDTH, 3 * SB_WIDTH + 2 * DIFF_WIDTH
QKV_COLS = 3 * SB_WIDTH + 3 * DIFF_WIDTH
SEG = 1024


def _nt_dot(a, b):
    return lax.dot_general(a, b, (((1,), (1,)), ((), ())), preferred_element_type=F32)


def _dot(a, b):
    return jnp.dot(a, b, preferred_element_type=F32)


def _rope_table(pos):
    half = ROT_DIM // 2
    inv = ROPE_THETA ** (-jnp.arange(half, dtype=F32) / half)
    ang = pos.astype(F32)[:, None] * inv[None, :]
    cos, sin = jnp.cos(ang), jnp.sin(ang)
    t = pos.shape[0]
    rest1 = jnp.ones((t, HEAD_DIM - ROT_DIM), F32)
    rest0 = jnp.zeros((t, HEAD_DIM - ROT_DIM), F32)
    z = jnp.zeros((t, half), F32)
    c = jnp.concatenate([cos, cos, rest1], axis=1)
    s1 = jnp.concatenate([-sin, z, rest0], axis=1)
    s2 = jnp.concatenate([z, sin, rest0], axis=1)
    return jnp.concatenate([c, s1, s2], axis=1)


def _rope(z, rope):
    c, s1, s2 = rope[:, :LANES], rope[:, LANES:2 * LANES], rope[:, 2 * LANES:]
    half = ROT_DIM // 2
    outs = []
    for i in range(z.shape[1] // LANES):
        zc = z[:, i * LANES:(i + 1) * LANES]
        outs.append(zc * c + pltpu.roll(zc, LANES - half, 1) * s1 + pltpu.roll(zc, half, 1) * s2)
    return jnp.concatenate(outs, axis=1)


def _proj_kernel(x_ref, w_ref, rope_ref, qkv_ref, sbkv_ref, dkv_ref, gate_ref, xb_ref, *, nb):
    j = pl.program_id(1)

    @pl.when(j == 0)
    def _():
        xb_ref[...] = x_ref[...].astype(BF16)

    z = _dot(xb_ref[...], w_ref[...])
    seg = j // nb

    @pl.when(seg == 0)
    def _():
        qkv_ref[...] = (z * QK_SCALE).astype(BF16)

    @pl.when((seg == 1) | (seg == 2))
    def _():
        qkv_ref[...] = z.astype(BF16)
        sbkv_ref[...] = z

    @pl.when(seg == 3)
    def _():
        qkv_ref[...] = (_rope(z, rope_ref[...]) * QK_SCALE).astype(BF16)

    @pl.when(seg == 4)
    def _():
        r = _rope(z, rope_ref[...])
        qkv_ref[...] = r.astype(BF16)
        dkv_ref[...] = r

    @pl.when(seg == 5)
    def _():
        qkv_ref[...] = z.astype(BF16)
        dkv_ref[...] = z

    @pl.when(seg >= 6)
    def _():
        gate_ref[...] = jax.nn.sigmoid(z).astype(BF16)


def _project(x, w_bf16, rope, *, bm, bn):
    m, d = x.shape
    n = w_bf16.shape[1]
    nb = SEG // bn
    grid = (m // bm, n // bn)
    return pl.pallas_call(
        functools.partial(_proj_kernel, nb=nb),
        grid=grid,
        in_specs=[
            pl.BlockSpec((bm, d), lambda i, j: (i, 0)),
            pl.BlockSpec((d, bn), lambda i, j: (0, j)),
            pl.BlockSpec((bm, 3 * LANES), lambda i, j: (i, 0)),
        ],
        out_specs=[
            pl.BlockSpec((bm, bn), lambda i, j: (i, jnp.minimum(j, 6 * nb - 1))),
            pl.BlockSpec((bm, bn), lambda i, j: (i, jnp.clip(j - nb, 0, 2 * nb - 1))),
            pl.BlockSpec((bm, bn), lambda i, j: (i, jnp.clip(j - 4 * nb, 0, 2 * nb - 1))),
            pl.BlockSpec((bm, bn), lambda i, j: (i, jnp.clip(j - 6 * nb, 0, 4 * nb - 1))),
        ],
        out_shape=[
            jax.ShapeDtypeStruct((m, QKV_COLS), BF16),
            jax.ShapeDtypeStruct((m, 2 * SB_WIDTH), F32),
            jax.ShapeDtypeStruct((m, 2 * DIFF_WIDTH), F32),
            jax.ShapeDtypeStruct((m, n - QKV_COLS), BF16),
        ],
        scratch_shapes=[pltpu.VMEM((bm, d), BF16)],
        compiler_params=pltpu.CompilerParams(
            dimension_semantics=("arbitrary", "arbitrary"), vmem_limit_bytes=VMEM_LIMIT),
        name="proj",
    )(x, w_bf16, rope)


def _split_dot(x, t):
    hi = x.astype(BF16)
    r1 = x - hi.astype(F32)
    mid = r1.astype(BF16)
    lo = (r1 - mid.astype(F32)).astype(BF16)
    return _dot(hi, t) + _dot(mid, t) + _dot(lo, t)


def _sb_block(q, k, v, tri, mask, c, acc):
    tk = k.shape[0]
    z = _nt_dot(q, k)
    lk = -(jnp.maximum(z, 0.0) + jnp.log1p(jnp.exp(-jnp.abs(z))))
    if mask is not None:
        lk = jnp.where(mask, lk, 0.0)
    s2 = _split_dot(lk, tri)
    w = jnp.exp(z + lk + s2[:, :tk] + c)
    if mask is not None:
        w = jnp.where(mask, w, 0.0)
    acc = acc + _dot(w.astype(BF16), v)
    return c + s2[:, tk:], acc


def _sb_attn_kernel(q_ref, k_ref, v_ref, km_ref, vm_ref, tri_ref, o_ref, *, bq):
    qb = pl.program_id(1)
    q = q_ref[...]
    tri = tri_ref[...]
    row = lax.broadcasted_iota(jnp.int32, (bq, bq), 0)
    col = lax.broadcasted_iota(jnp.int32, (bq, bq), 1)
    zeros = jnp.zeros((bq, bq), F32)

    start = pl.multiple_of(qb * bq, bq)
    c, acc = _sb_block(q, k_ref[pl.ds(start, bq), :], v_ref[pl.ds(start, bq), :], tri, col < row,
                       zeros, jnp.zeros((bq, HEAD_DIM), F32))

    def body(i, carry):
        s = pl.multiple_of((qb - 1 - i) * bq, bq)
        return _sb_block(q, k_ref[pl.ds(s, bq), :], v_ref[pl.ds(s, bq), :], tri, None, *carry)

    c, acc = lax.fori_loop(0, qb, body, (c, acc))
    c, acc = _sb_block(q, km_ref[...], vm_ref[...], tri, col < N_META, c, acc)
    o_ref[...] = acc.astype(o_ref.dtype)


def _sb_attention(qkv, qkv_meta, tri, *, bq):
    t = qkv.shape[0]
    kb, vb = COL_SBK // HEAD_DIM, COL_SBV // HEAD_DIM
    return pl.pallas_call(
        functools.partial(_sb_attn_kernel, bq=bq),
        grid=(N_SB_HEADS, t // bq),
        in_specs=[
            pl.BlockSpec((bq, HEAD_DIM), lambda h, i: (i, h)),
            pl.BlockSpec((t, HEAD_DIM), lambda h, i: (0, kb + h)),
            pl.BlockSpec((t, HEAD_DIM), lambda h, i: (0, vb + h)),
            pl.BlockSpec((bq, HEAD_DIM), lambda h, i: (0, kb + h)),
            pl.BlockSpec((bq, HEAD_DIM), lambda h, i: (0, vb + h)),
            pl.BlockSpec((bq, 2 * bq), lambda h, i: (0, 0)),
        ],
        out_specs=pl.BlockSpec((bq, HEAD_DIM), lambda h, i: (i, h)),
        out_shape=jax.ShapeDtypeStruct((t, SB_WIDTH), BF16),
        compiler_params=pltpu.CompilerParams(
            dimension_semantics=("arbitrary", "arbitrary"), vmem_limit_bytes=VMEM_LIMIT),
        name="sb_attn",
    )(qkv, qkv, qkv, qkv_meta, qkv_meta, tri)


def _lambda(lq1_ref, lk1_ref, lq2_ref, lk2_ref):
    a = jnp.sum(lq1_ref[...] * lk1_ref[...], axis=-1, keepdims=True)
    b = jnp.sum(lq2_ref[...] * lk2_ref[...], axis=-1, keepdims=True)
    return jnp.exp(a) - jnp.exp(b) + LAM_INIT


def _softmax_block(q, k, v, mask, m, l, acc):
    s = _nt_dot(q, k)
    if mask is not None:
        s = jnp.where(mask, s, -jnp.inf)
    m_new = jnp.maximum(m, jnp.max(s, axis=-1, keepdims=True))
    a = jnp.exp(m - m_new)
    p = jnp.exp(s - m_new)
    l = a * l + jnp.sum(p, axis=-1, keepdims=True)
    acc = a * acc + _dot(p.astype(BF16), v)
    return m_new, l, acc


def _head_norm(o, g):
    of = o * lax.rsqrt(jnp.mean(jnp.square(o), axis=-1, keepdims=True) + LN_EPS)
    return (of * g) * (1.0 - LAM_INIT)


def _diff_attn_kernel(lq1_ref, lk1_ref, lq2_ref, lk2_ref, g_ref, q1_ref, q2_ref, k1_ref, k2_ref, v_ref,
                      km1_ref, km2_ref, vm_ref, o_ref, *, bq):
    qb = pl.program_id(1)
    lam = _lambda(lq1_ref, lk1_ref, lq2_ref, lk2_ref)
    row = lax.broadcasted_iota(jnp.int32, (bq, bq), 0)
    col = lax.broadcasted_iota(jnp.int32, (bq, bq), 1)
    start = pl.multiple_of(qb * bq, bq)
    outs = []
    for q_ref, k_ref, km_ref in ((q1_ref, k1_ref, km1_ref), (q2_ref, k2_ref, km2_ref)):
        q = q_ref[...]
        state = (jnp.full((bq, 1), -jnp.inf, F32), jnp.zeros((bq, 1), F32), jnp.zeros((bq, 2 * HEAD_DIM), F32))
        state = _softmax_block(q, km_ref[...], vm_ref[...], col < N_META, *state)

        def body(i, carry, q=q, k_ref=k_ref):
            s = pl.multiple_of(i * bq, bq)
            return _softmax_block(q, k_ref[pl.ds(s, bq), :], v_ref[pl.ds(s, bq), :], None, *carry)

        state = lax.fori_loop(0, qb, body, state)
        m, l, acc = _softmax_block(q, k_ref[pl.ds(start, bq), :], v_ref[pl.ds(start, bq), :], col <= row, *state)
        outs.append(acc / l)
    o = outs[0] - lam * outs[1]
    o_ref[...] = _head_norm(o, g_ref[...]).astype(o_ref.dtype)


def _diff_attention(qkv, qkv_meta, lams, diff_norm_g, *, bq):
    t = qkv.shape[0]
    qb, kb, vb = COL_DQ // HEAD_DIM, COL_DK // HEAD_DIM, COL_DV // (2 * HEAD_DIM)
    vec = pl.BlockSpec((1, HEAD_DIM), lambda h, i: (0, 0))
    return pl.pallas_call(
        functools.partial(_diff_attn_kernel, bq=bq),
        grid=(N_DIFF_HEADS, t // bq),
        in_specs=[
            vec, vec, vec, vec,
            pl.BlockSpec((1, 2 * HEAD_DIM), lambda h, i: (0, 0)),
            pl.BlockSpec((bq, HEAD_DIM), lambda h, i: (i, qb + 2 * h)),
            pl.BlockSpec((bq, HEAD_DIM), lambda h, i: (i, qb + 2 * h + 1)),
            pl.BlockSpec((t, HEAD_DIM), lambda h, i: (0, kb + 2 * h)),
            pl.BlockSpec((t, HEAD_DIM), lambda h, i: (0, kb + 2 * h + 1)),
            pl.BlockSpec((t, 2 * HEAD_DIM), lambda h, i: (0, vb + h)),
            pl.BlockSpec((bq, HEAD_DIM), lambda h, i: (0, kb + 2 * h)),
            pl.BlockSpec((bq, HEAD_DIM), lambda h, i: (0, kb + 2 * h + 1)),
            pl.BlockSpec((bq, 2 * HEAD_DIM), lambda h, i: (0, vb + h)),
        ],
        out_specs=pl.BlockSpec((bq, 2 * HEAD_DIM), lambda h, i: (i, h)),
        out_shape=jax.ShapeDtypeStruct((t, DIFF_WIDTH), BF16),
        compiler_params=pltpu.CompilerParams(
            dimension_semantics=("arbitrary", "arbitrary"), vmem_limit_bytes=VMEM_LIMIT),
        name="diff_attn",
    )(*lams, diff_norm_g, qkv, qkv, qkv, qkv, qkv, qkv_meta, qkv_meta, qkv_meta)


def _layer_norm(x, g, b):
    mu = jnp.mean(x, axis=-1, keepdims=True)
    xc = x - mu
    var = jnp.mean(jnp.square(xc), axis=-1, keepdims=True)
    return (xc * lax.rsqrt(var + LN_EPS)) * g + b


def _merge_kernel(x_ref, osb_ref, od_ref, gsb_ref, gd_ref, wsb_ref, wd_ref, wo_ref, g_ref, b_ref, o_ref):
    br_sb = _dot(osb_ref[...], wsb_ref[...])
    br_d = _dot(od_ref[...], wd_ref[...])
    gated = gsb_ref[...].astype(F32) * br_sb + gd_ref[...].astype(F32) * br_d
    mix = _dot(gated.astype(BF16), wo_ref[...])
    o_ref[...] = _layer_norm(ALPHA * x_ref[...] + mix, g_ref[...], b_ref[...])


def _merge(x, o_sb, o_d, gates, w_sb, w_d, w_o, ln_g, ln_b, *, bm):
    m, d = x.shape
    row = lambda i: (i, 0)
    const = lambda i: (0, 0)
    return pl.pallas_call(
        _merge_kernel,
        grid=(m // bm,),
        in_specs=[
            pl.BlockSpec((bm, d), row),
            pl.BlockSpec((bm, SB_WIDTH), row),
            pl.BlockSpec((bm, DIFF_WIDTH), row),
            pl.BlockSpec((bm, d), row),
            pl.BlockSpec((bm, d), lambda i: (i, 1)),
            pl.BlockSpec((SB_WIDTH, d), const),
            pl.BlockSpec((DIFF_WIDTH, d), const),
            pl.BlockSpec((d, d), const),
            pl.BlockSpec((1, d), const),
            pl.BlockSpec((1, d), const),
        ],
        out_specs=pl.BlockSpec((bm, d), row),
        out_shape=jax.ShapeDtypeStruct((m, d), F32),
        compiler_params=pltpu.CompilerParams(
            dimension_semantics=("arbitrary",), vmem_limit_bytes=VMEM_LIMIT),
        name="merge",
    )(x, o_sb, o_d, gates, gates, w_sb, w_d, w_o, ln_g, ln_b)


def _ffn_kernel(x_ref, wu_ref, wd_ref, g_ref, b_ref, o_ref, xb_ref, acc_ref):
    f = pl.program_id(1)

    @pl.when(f == 0)
    def _():
        xb_ref[...] = x_ref[...].astype(BF16)
        acc_ref[...] = jnp.zeros_like(acc_ref)

    u = _dot(xb_ref[...], wu_ref[...])
    a = jnp.square(jnp.maximum(u, 0.0)).astype(BF16)
    acc_ref[...] += _dot(a, wd_ref[...])

    @pl.when(f == pl.num_programs(1) - 1)
    def _():
        o_ref[...] = _layer_norm(ALPHA * x_ref[...] + acc_ref[...], g_ref[...], b_ref[...])


def _ffn(x, w_up, w_down, ln_g, ln_b, *, bm, bf):
    m, d = x.shape
    dff = w_up.shape[1]
    return pl.pallas_call(
        _ffn_kernel,
        grid=(m // bm, dff // bf),
        in_specs=[
            pl.BlockSpec((bm, d), lambda i, f: (i, 0)),
            pl.BlockSpec((d, bf), lambda i, f: (0, f)),
            pl.BlockSpec((bf, d), lambda i, f: (f, 0)),
            pl.BlockSpec((1, d), lambda i, f: (0, 0)),
            pl.BlockSpec((1, d), lambda i, f: (0, 0)),
        ],
        out_specs=pl.BlockSpec((bm, d), lambda i, f: (i, 0)),
        out_shape=jax.ShapeDtypeStruct((m, d), F32),
        scratch_shapes=[pltpu.VMEM((bm, d), BF16), pltpu.VMEM((bm, d), F32)],
        compiler_params=pltpu.CompilerParams(
            dimension_semantics=("arbitrary", "arbitrary"), vmem_limit_bytes=VMEM_LIMIT),
        name="ffn",
    )(x, w_up, w_down, ln_g, ln_b)


N_ROWS = 8


def _sample_attn_kernel(pt_ref, lq1_ref, lk1_ref, lq2_ref, lk2_ref, g_ref, q_ref, kvs_ref, tri_ref,
                        sbc_ref, dc_ref, osb_ref, od_ref, c_ref, asb_ref, m_ref, l_ref, ad_ref):
    del pt_ref
    p = pl.program_id(1)
    page = tri_ref.shape[0]
    rows = lax.broadcasted_iota(jnp.int32, (N_ROWS, SEG), 0)
    cols = lax.broadcasted_iota(jnp.int32, (N_ROWS, SEG), 1)
    blockdiag = (cols // HEAD_DIM) == rows
    q_all = q_ref[0].astype(F32)
    q_sb = jnp.where(blockdiag, q_all[:, COL_SBQ:COL_SBQ + SEG], 0.0).astype(BF16)
    q_d = jnp.where(blockdiag, q_all[:, COL_DQ:COL_DQ + SEG], 0.0).astype(BF16)

    @pl.when(p == 0)
    def _():
        c_ref[...] = jnp.zeros_like(c_ref)
        asb_ref[...] = jnp.zeros_like(asb_ref)
        kv_self = kvs_ref[0]
        s_self = jnp.sum(q_d.astype(F32) * kv_self[:, :SEG].astype(F32), axis=-1, keepdims=True)
        m_ref[...] = jnp.broadcast_to(s_self, m_ref.shape)
        l_ref[...] = jnp.ones_like(l_ref)
        ad_ref[...] = jnp.broadcast_to(kv_self[:, SEG:].astype(F32), ad_ref.shape)

    sb = sbc_ref[0]
    z = _nt_dot(q_sb, sb[:, :SEG].astype(BF16))
    lk = -(jnp.maximum(z, 0.0) + jnp.log1p(jnp.exp(-jnp.abs(z))))
    s2 = _split_dot(lk, tri_ref[...])
    c = c_ref[...]
    w = jnp.exp(z + lk + s2[:, :page] + c)
    asb_ref[...] += _dot(w.astype(BF16), sb[:, SEG:].astype(BF16))
    c_ref[...] = c + s2[:, page:]

    dc = dc_ref[0]
    s = _nt_dot(q_d, dc[:, :SEG].astype(BF16))
    m = m_ref[...][:, :1]
    m_new = jnp.maximum(m, jnp.max(s, axis=-1, keepdims=True))
    a = jnp.exp(m - m_new)
    pr = jnp.exp(s - m_new)
    l_ref[...] = a * l_ref[...] + jnp.sum(pr, axis=-1, keepdims=True)
    ad_ref[...] = a * ad_ref[...] + _dot(pr.astype(BF16), dc[:, SEG:].astype(BF16))
    m_ref[...] = jnp.broadcast_to(m_new, m_ref.shape)

    @pl.when(p == pl.num_programs(1) - 1)
    def _():
        osb_ref[0] = jnp.sum(jnp.where(blockdiag, asb_ref[...], 0.0), axis=0, keepdims=True).astype(osb_ref.dtype)
        lam = _lambda(lq1_ref, lk1_ref, lq2_ref, lk2_ref)
        coef = jnp.where(rows % 2 == 0, 1.0, -lam)
        own_head = (cols // (2 * HEAD_DIM)) == (rows // 2)
        norm = ad_ref[...] / l_ref[...][:, :1]
        o = jnp.sum(jnp.where(own_head, coef * norm, 0.0), axis=0, keepdims=True)
        g = g_ref[...]
        heads = [_head_norm(o[:, h * 2 * HEAD_DIM:(h + 1) * 2 * HEAD_DIM], g) for h in range(N_DIFF_HEADS)]
        od_ref[0] = jnp.concatenate(heads, axis=1).astype(od_ref.dtype)


def _sample_attention(page_table, qkv_s, cache_sb, cache_d, tri, lams, diff_norm_g):
    b, n_pages = page_table.shape
    page = cache_sb.shape[1]
    q3 = qkv_s.reshape(b, 1, QKV_COLS)
    kv_self = qkv_s[:, COL_DK:].reshape(b, 1, 2 * DIFF_WIDTH)
    vec = pl.BlockSpec((1, HEAD_DIM), lambda i, p, pt: (0, 0))
    newest_first = lambda i, p, pt: (pt[i, n_pages - 1 - p], 0, 0)
    per_seq = lambda i, p, pt: (i, 0, 0)
    grid_spec = pltpu.PrefetchScalarGridSpec(
        num_scalar_prefetch=1,
        grid=(b, n_pages),
        in_specs=[
            vec, vec, vec, vec,
            pl.BlockSpec((1, 2 * HEAD_DIM), lambda i, p, pt: (0, 0)),
            pl.BlockSpec((1, 1, QKV_COLS), per_seq),
            pl.BlockSpec((1, 1, 2 * DIFF_WIDTH), per_seq),
            pl.BlockSpec((page, 2 * page), lambda i, p, pt: (0, 0)),
            pl.BlockSpec((1, page, 2 * SB_WIDTH), newest_first),
            pl.BlockSpec((1, page, 2 * DIFF_WIDTH), newest_first),
        ],
        out_specs=[
            pl.BlockSpec((1, 1, SB_WIDTH), per_seq),
            pl.BlockSpec((1, 1, DIFF_WIDTH), per_seq),
        ],
        scratch_shapes=[
            pltpu.VMEM((N_ROWS, page), F32),
            pltpu.VMEM((N_ROWS, SB_WIDTH), F32),
            pltpu.VMEM((N_ROWS, LANES), F32),
            pltpu.VMEM((N_ROWS, LANES), F32),
            pltpu.VMEM((N_ROWS, DIFF_WIDTH), F32),
        ],
    )
    o_sb, o_d = pl.pallas_call(
        _sample_attn_kernel,
        grid_spec=grid_spec,
        out_shape=[
            jax.ShapeDtypeStruct((b, 1, SB_WIDTH), BF16),
            jax.ShapeDtypeStruct((b, 1, DIFF_WIDTH), BF16),
        ],
        compiler_params=pltpu.CompilerParams(
            dimension_semantics=("arbitrary", "arbitrary"), vmem_limit_bytes=VMEM_LIMIT),
        name="sample_attn",
    )(page_table, *lams, diff_norm_g, q3, kv_self, tri, cache_sb, cache_d)
    return o_sb.reshape(b, SB_WIDTH), o_d.reshape(b, DIFF_WIDTH)


def _suffix_matrix(n):
    j = jnp.arange(n)[:, None]
    s = jnp.arange(n)[None, :]
    return jnp.concatenate([(j > s), jnp.ones((n, n), bool)], axis=1).astype(BF16)


def _pick(m, candidates):
    for c in candidates:
        if m % c == 0:
            return c
    raise ValueError(f"no block size for {m}")


def kernel(x_prompt, x_sample, cache_sb_kv, cache_diff_kv, page_table, meta_tokens, w_in, lambda_q1, lambda_k1,
           lambda_q2, lambda_k2, diff_norm_g, w_branch_sb, w_branch_diff, w_out, ln_mix_g, ln_mix_b, w_up,
           w_down, ln_ffn_g, ln_ffn_b):
    assert w_in.shape[0] == DEPTH and x_prompt.shape[0] == 1 and x_sample.shape[1] == 1
    seq, d = x_prompt.shape[1:]
    n_dec = x_sample.shape[0]
    n_pool, page = cache_sb_kv.shape[1:3]
    past_len = page_table.shape[1] * page
    bq = 128

    w_in_b = w_in[0].astype(BF16)
    w_sb_b, w_d_b, w_o_b = w_branch_sb[0].astype(BF16), w_branch_diff[0].astype(BF16), w_out[0].astype(BF16)
    w_up_b, w_down_b = w_up[0].astype(BF16), w_down[0].astype(BF16)
    lams = (lambda_q1, lambda_k1, lambda_q2, lambda_k2)

    xp = x_prompt[0]
    rope_p = _rope_table(N_META + jnp.arange(seq))
    qkv_p, sbkv_p, dkv_p, gates_p = _project(xp, w_in_b, rope_p, bm=_pick(seq, (1024, 512, 256, 128)), bn=512)
    x_small = jnp.concatenate([meta_tokens.astype(F32), x_sample[:, 0]], axis=0)
    pos_small = jnp.concatenate([jnp.arange(N_META), jnp.full((n_dec,), past_len)])
    qkv_s, sbkv_s, dkv_s, gates_s = _project(x_small, w_in_b, _rope_table(pos_small), bm=N_META + n_dec, bn=512)

    tri = _suffix_matrix(bq)
    qkv_meta = jnp.concatenate([qkv_s[:N_META], jnp.zeros((bq - N_META, QKV_COLS), BF16)], axis=0)
    o_sb = _sb_attention(qkv_p, qkv_meta, tri, bq=bq)
    o_d = _diff_attention(qkv_p, qkv_meta, lams, diff_norm_g, bq=bq)
    x1 = _merge(xp, o_sb, o_d, gates_p, w_sb_b, w_d_b, w_o_b, ln_mix_g, ln_mix_b, bm=_pick(seq, (256, 128)))
    y_prompt = _ffn(x1, w_up_b, w_down_b, ln_ffn_g, ln_ffn_b, bm=_pick(seq, (512, 256, 128)), bf=512)

    cache_sb = cache_sb_kv[0].reshape(n_pool, page, 2 * SB_WIDTH)
    cache_d = cache_diff_kv[0].reshape(n_pool, page, 2 * DIFF_WIDTH)
    os_sb, os_d = _sample_attention(page_table, qkv_s[N_META:], cache_sb, cache_d, _suffix_matrix(page), lams,
                                    diff_norm_g)
    xs1 = _merge(x_sample[:, 0], os_sb, os_d, gates_s[N_META:], w_sb_b, w_d_b, w_o_b, ln_mix_g, ln_mix_b,
                 bm=n_dec)
    y_sample = _ffn(xs1, w_up_b, w_down_b, ln_ffn_g, ln_ffn_b, bm=n_dec, bf=512)

    t_p = N_META + seq
    sb_kv_prompt = jnp.concatenate([sbkv_s[:N_META], sbkv_p], axis=0).reshape(1, 1, t_p, 2, N_SB_HEADS, HEAD_DIM)
    diff_kv_prompt = jnp.concatenate([dkv_s[:N_META], dkv_p], axis=0).reshape(
        1, 1, t_p, 2, N_DIFF_HEADS, 2 * HEAD_DIM)
    sb_kv_sample = sbkv_s[N_META:].reshape(1, n_dec, 1, 2, N_SB_HEADS, HEAD_DIM)
    diff_kv_sample = dkv_s[N_META:].reshape(1, n_dec, 1, 2, N_DIFF_HEADS, 2 * HEAD_DIM)
    return (y_prompt[None], y_sample[:, None], sb_kv_prompt, diff_kv_prompt, sb_kv_sample, diff_kv_sample)
```

```python
import functools
import math

import jax
import jax.numpy as jnp
from jax import lax
from jax.experimental import pallas as pl
from jax.experimental.pallas import tpu as pltpu

F32 = jnp.float32
BF16 = jnp.bfloat16

HEAD_DIM = 128
N_SB_HEADS = 8
N_DIFF_HEADS = 4
SB_WIDTH = N_SB_HEADS * HEAD_DIM
DIFF_WIDTH = N_DIFF_HEADS * 2 * HEAD_DIM
N_META = 16
ROT_DIM = HEAD_DIM // 4
ROPE_THETA = 500000.0
LN_EPS = 1e-5
DEPTH = 1
ALPHA = (2 * DEPTH) ** 0.25
LAM_INIT = 0.8 - 0.6 * math.exp(-0.3 * 0)
QK_SCALE = HEAD_DIM ** -0.5
LANES = 128
VMEM_LIMIT = 56 * 1024 * 1024

COL_SBQ, COL_SBK, COL_SBV = 0, SB_WIDTH, 2 * SB_WIDTH
COL_DQ, COL_DK, COL_DV = 3 * SB_WIDTH, 3 * SB_WIDTH + DIFF_WIDTH, 3 * SB_WIDTH + 2 * DIFF_WIDTH
QKV_COLS = 3 * SB_WIDTH + 3 * DIFF_WIDTH
SEG = 1024


def _nt_dot(a, b):
    return lax.dot_general(a, b, (((1,), (1,)), ((), ())), preferred_element_type=F32)


def _dot(a, b):
    return jnp.dot(a, b, preferred_element_type=F32)


def _rope_table(pos):
    half = ROT_DIM // 2
    inv = ROPE_THETA ** (-jnp.arange(half, dtype=F32) / half)
    ang = pos.astype(F32)[:, None] * inv[None, :]
    cos, sin = jnp.cos(ang), jnp.sin(ang)
    t = pos.shape[0]
    rest1 = jnp.ones((t, HEAD_DIM - ROT_DIM), F32)
    rest0 = jnp.zeros((t, HEAD_DIM - ROT_DIM), F32)
    z = jnp.zeros((t, half), F32)
    c = jnp.concatenate([cos, cos, rest1], axis=1)
    s1 = jnp.concatenate([-sin, z, rest0], axis=1)
    s2 = jnp.concatenate([z, sin, rest0], axis=1)
    return jnp.concatenate([c, s1, s2], axis=1)


def _rope(z, rope):
    c, s1, s2 = rope[:, :LANES], rope[:, LANES:2 * LANES], rope[:, 2 * LANES:]
    half = ROT_DIM // 2
    outs = []
    for i in range(z.shape[1] // LANES):
        zc = z[:, i * LANES:(i + 1) * LANES]
        outs.append(zc * c + pltpu.roll(zc, LANES - half, 1) * s1 + pltpu.roll(zc, half, 1) * s2)
    return jnp.concatenate(outs, axis=1)


def _proj_kernel(x_ref, w_ref, rope_ref, qkv_ref, sbkv_ref, dkv_ref, gate_ref, xb_ref, *, nb):
    j = pl.program_id(1)

    @pl.when(j == 0)
    def _():
        xb_ref[...] = x_ref[...].astype(BF16)

    z = _dot(xb_ref[...], w_ref[...])
    seg = j // nb

    @pl.when(seg == 0)
    def _():
        qkv_ref[...] = (z * QK_SCALE).astype(BF16)

    @pl.when((seg == 1) | (seg == 2))
    def _():
        qkv_ref[...] = z.astype(BF16)
        sbkv_ref[...] = z

    @pl.when(seg == 3)
    def _():
        qkv_ref[...] = (_rope(z, rope_ref[...]) * QK_SCALE).astype(BF16)

    @pl.when(seg == 4)
    def _():
        r = _rope(z, rope_ref[...])
        qkv_ref[...] = r.astype(BF16)
        dkv_ref[...] = r

    @pl.when(seg == 5)
    def _():
        qkv_ref[...] = z.astype(BF16)
        dkv_ref[...] = z

    @pl.when(seg >= 6)
    def _():
        gate_ref[...] = jax.nn.sigmoid(z).astype(BF16)


def _project(x, w_bf16, rope, *, bm, bn):
    m, d = x.shape
    n = w_bf16.shape[1]
    nb = SEG // bn
    grid = (m // bm, n // bn)
    return pl.pallas_call(
        functools.partial(_proj_kernel, nb=nb),
        grid=grid,
        in_specs=[
            pl.BlockSpec((bm, d), lambda i, j: (i, 0)),
            pl.BlockSpec((d, bn), lambda i, j: (0, j)),
            pl.BlockSpec((bm, 3 * LANES), lambda i, j: (i, 0)),
        ],
        out_specs=[
            pl.BlockSpec((bm, bn), lambda i, j: (i, jnp.minimum(j, 6 * nb - 1))),
            pl.BlockSpec((bm, bn), lambda i, j: (i, jnp.clip(j - nb, 0, 2 * nb - 1))),
            pl.BlockSpec((bm, bn), lambda i, j: (i, jnp.clip(j - 4 * nb, 0, 2 * nb - 1))),
            pl.BlockSpec((bm, bn), lambda i, j: (i, jnp.clip(j - 6 * nb, 0, 4 * nb - 1))),
        ],
        out_shape=[
            jax.ShapeDtypeStruct((m, QKV_COLS), BF16),
            jax.ShapeDtypeStruct((m, 2 * SB_WIDTH), F32),
            jax.ShapeDtypeStruct((m, 2 * DIFF_WIDTH), F32),
            jax.ShapeDtypeStruct((m, n - QKV_COLS), BF16),
        ],
        scratch_shapes=[pltpu.VMEM((bm, d), BF16)],
        compiler_params=pltpu.CompilerParams(
            dimension_semantics=("arbitrary", "arbitrary"), vmem_limit_bytes=VMEM_LIMIT),
        name="proj",
    )(x, w_bf16, rope)


def _split_dot(x, t):
    hi = x.astype(BF16)
    r1 = x - hi.astype(F32)
    mid = r1.astype(BF16)
    lo = (r1 - mid.astype(F32)).astype(BF16)
    return _dot(hi, t) + _dot(mid, t) + _dot(lo, t)


def _sb_block(q, k, v, tri, mask, c, acc):
    tk = k.shape[0]
    z = _nt_dot(q, k)
    lk = -(jnp.maximum(z, 0.0) + jnp.log1p(jnp.exp(-jnp.abs(z))))
    if mask is not None:
        lk = jnp.where(mask, lk, 0.0)
    s2 = _split_dot(lk, tri)
    w = jnp.exp(z + lk + s2[:, :tk] + c)
    if mask is not None:
        w = jnp.where(mask, w, 0.0)
    acc = acc + _dot(w.astype(BF16), v)
    return c + s2[:, tk:], acc


LOG_F32_ZERO = -104.0


def _sb_attn_kernel(q_ref, k_ref, v_ref, km_ref, vm_ref, tri_ref, o_ref, c_ref, acc_ref, *, bq):
    qb = pl.program_id(1)
    q = q_ref[...]
    tri = tri_ref[...]
    row = lax.broadcasted_iota(jnp.int32, (bq, bq), 0)
    col = lax.broadcasted_iota(jnp.int32, (bq, bq), 1)

    start = pl.multiple_of(qb * bq, bq)
    c, acc = _sb_block(q, k_ref[pl.ds(start, bq), :], v_ref[pl.ds(start, bq), :], tri, col < row,
                       jnp.zeros((bq, bq), F32), jnp.zeros((bq, HEAD_DIM), F32))
    c_ref[...] = c
    acc_ref[...] = acc

    def live(carry):
        kb, c_max = carry
        return (kb >= 0) & (c_max > LOG_F32_ZERO)

    def body(carry):
        kb, _ = carry
        s = pl.multiple_of(kb * bq, bq)
        c, acc = _sb_block(q, k_ref[pl.ds(s, bq), :], v_ref[pl.ds(s, bq), :], tri, None, c_ref[...], acc_ref[...])
        c_ref[...] = c
        acc_ref[...] = acc
        return kb - 1, jnp.max(c)

    _, c_max = lax.while_loop(live, body, (qb - 1, jnp.max(c)))

    @pl.when(c_max > LOG_F32_ZERO)
    def _():
        _, acc = _sb_block(q, km_ref[...], vm_ref[...], tri, col < N_META, c_ref[...], acc_ref[...])
        acc_ref[...] = acc

    o_ref[...] = acc_ref[...].astype(o_ref.dtype)


def _sb_attention(qkv, qkv_meta, tri, *, bq):
    t = qkv.shape[0]
    kb, vb = COL_SBK // HEAD_DIM, COL_SBV // HEAD_DIM
    return pl.pallas_call(
        functools.partial(_sb_attn_kernel, bq=bq),
        grid=(N_SB_HEADS, t // bq),
        in_specs=[
            pl.BlockSpec((bq, HEAD_DIM), lambda h, i: (i, h)),
            pl.BlockSpec((t, HEAD_DIM), lambda h, i: (0, kb + h)),
            pl.BlockSpec((t, HEAD_DIM), lambda h, i: (0, vb + h)),
            pl.BlockSpec((bq, HEAD_DIM), lambda h, i: (0, kb + h)),
            pl.BlockSpec((bq, HEAD_DIM), lambda h, i: (0, vb + h)),
            pl.BlockSpec((bq, 2 * bq), lambda h, i: (0, 0)),
        ],
        out_specs=pl.BlockSpec((bq, HEAD_DIM), lambda h, i: (i, h)),
        out_shape=jax.ShapeDtypeStruct((t, SB_WIDTH), BF16),
        scratch_shapes=[pltpu.VMEM((bq, bq), F32), pltpu.VMEM((bq, HEAD_DIM), F32)],
        compiler_params=pltpu.CompilerParams(
            dimension_semantics=("arbitrary", "arbitrary"), vmem_limit_bytes=VMEM_LIMIT),
        name="sb_attn",
    )(qkv, qkv, qkv, qkv_meta, qkv_meta, tri)


def _lambda(lq1_ref, lk1_ref, lq2_ref, lk2_ref):
    a = jnp.sum(lq1_ref[...] * lk1_ref[...], axis=-1, keepdims=True)
    b = jnp.sum(lq2_ref[...] * lk2_ref[...], axis=-1, keepdims=True)
    return jnp.exp(a) - jnp.exp(b) + LAM_INIT


def _softmax_block(q, k, v, mask, m, l, acc):
    s = _nt_dot(q, k)
    if mask is not None:
        s = jnp.where(mask, s, -jnp.inf)
    m_new = jnp.maximum(m, jnp.max(s, axis=-1, keepdims=True))
    a = jnp.exp(m - m_new)
    p = jnp.exp(s - m_new)
    l = a * l + jnp.sum(p, axis=-1, keepdims=True)
    acc = a * acc + _dot(p.astype(BF16), v)
    return m_new, l, acc


def _head_norm(o, g):
    of = o * lax.rsqrt(jnp.mean(jnp.square(o), axis=-1, keepdims=True) + LN_EPS)
    return (of * g) * (1.0 - LAM_INIT)


def _diff_attn_kernel(lq1_ref, lk1_ref, lq2_ref, lk2_ref, g_ref, q1_ref, q2_ref, k1_ref, k2_ref, v_ref,
                      km1_ref, km2_ref, vm_ref, o_ref, *, bq, bk):
    qb = pl.program_id(1)
    lam = _lambda(lq1_ref, lk1_ref, lq2_ref, lk2_ref)
    q1, q2 = q1_ref[...], q2_ref[...]
    n_meta_rows = km1_ref.shape[0]

    def both(k1, k2, v, mask, state):
        return _softmax_block(q1, k1, v, mask, *state[:3]) + _softmax_block(q2, k2, v, mask, *state[3:])

    init = (jnp.full((bq, 1), -jnp.inf, F32), jnp.zeros((bq, 1), F32), jnp.zeros((bq, 2 * HEAD_DIM), F32))
    meta_mask = lax.broadcasted_iota(jnp.int32, (bq, n_meta_rows), 1) < N_META
    state = both(km1_ref[...], km2_ref[...], vm_ref[...], meta_mask, init + init)

    def body(i, state):
        s = pl.multiple_of(i * bk, bk)
        return both(k1_ref[pl.ds(s, bk), :], k2_ref[pl.ds(s, bk), :], v_ref[pl.ds(s, bk), :], None, state)

    state = lax.fori_loop(0, qb * (bq // bk), body, state)

    row = lax.broadcasted_iota(jnp.int32, (bq, bk), 0)
    col = lax.broadcasted_iota(jnp.int32, (bq, bk), 1)
    for j in range(bq // bk):
        s = pl.multiple_of(qb * bq + j * bk, bk)
        state = both(k1_ref[pl.ds(s, bk), :], k2_ref[pl.ds(s, bk), :], v_ref[pl.ds(s, bk), :],
                     col + j * bk <= row, state)
    _, l1, acc1, _, l2, acc2 = state
    o = acc1 / l1 - lam * (acc2 / l2)
    o_ref[...] = _head_norm(o, g_ref[...]).astype(o_ref.dtype)


def _diff_attention(qkv, qkv_meta, lams, diff_norm_g, *, bq, bk):
    t = qkv.shape[0]
    mrows = qkv_meta.shape[0]
    qb, kb, vb = COL_DQ // HEAD_DIM, COL_DK // HEAD_DIM, COL_DV // (2 * HEAD_DIM)
    vec = pl.BlockSpec((1, HEAD_DIM), lambda h, i: (0, 0))
    return pl.pallas_call(
        functools.partial(_diff_attn_kernel, bq=bq, bk=bk),
        grid=(N_DIFF_HEADS, t // bq),
        in_specs=[
            vec, vec, vec, vec,
            pl.BlockSpec((1, 2 * HEAD_DIM), lambda h, i: (0, 0)),
            pl.BlockSpec((bq, HEAD_DIM), lambda h, i: (i, qb + 2 * h)),
            pl.BlockSpec((bq, HEAD_DIM), lambda h, i: (i, qb + 2 * h + 1)),
            pl.BlockSpec((t, HEAD_DIM), lambda h, i: (0, kb + 2 * h)),
            pl.BlockSpec((t, HEAD_DIM), lambda h, i: (0, kb + 2 * h + 1)),
            pl.BlockSpec((t, 2 * HEAD_DIM), lambda h, i: (0, vb + h)),
            pl.BlockSpec((mrows, HEAD_DIM), lambda h, i: (0, kb + 2 * h)),
            pl.BlockSpec((mrows, HEAD_DIM), lambda h, i: (0, kb + 2 * h + 1)),
            pl.BlockSpec((mrows, 2 * HEAD_DIM), lambda h, i: (0, vb + h)),
        ],
        out_specs=pl.BlockSpec((bq, 2 * HEAD_DIM), lambda h, i: (i, h)),
        out_shape=jax.ShapeDtypeStruct((t, DIFF_WIDTH), BF16),
        compiler_params=pltpu.CompilerParams(
            dimension_semantics=("arbitrary", "arbitrary"), vmem_limit_bytes=VMEM_LIMIT),
        name="diff_attn",
    )(*lams, diff_norm_g, qkv, qkv, qkv, qkv, qkv, qkv_meta, qkv_meta, qkv_meta)


def _layer_norm(x, g, b):
    mu = jnp.mean(x, axis=-1, keepdims=True)
    xc = x - mu
    var = jnp.mean(jnp.square(xc), axis=-1, keepdims=True)
    return (xc * lax.rsqrt(var + LN_EPS)) * g + b


def _merge_kernel(x_ref, osb_ref, od_ref, gsb_ref, gd_ref, wsb_ref, wd_ref, wo_ref, g_ref, b_ref, o_ref):
    br_sb = _dot(osb_ref[...], wsb_ref[...])
    br_d = _dot(od_ref[...], wd_ref[...])
    gated = gsb_ref[...].astype(F32) * br_sb + gd_ref[...].astype(F32) * br_d
    mix = _dot(gated.astype(BF16), wo_ref[...])
    o_ref[...] = _layer_norm(ALPHA * x_ref[...] + mix, g_ref[...], b_ref[...])


def _merge(x, o_sb, o_d, gates, w_sb, w_d, w_o, ln_g, ln_b, *, bm):
    m, d = x.shape
    row = lambda i: (i, 0)
    const = lambda i: (0, 0)
    return pl.pallas_call(
        _merge_kernel,
        grid=(m // bm,),
        in_specs=[
            pl.BlockSpec((bm, d), row),
            pl.BlockSpec((bm, SB_WIDTH), row),
            pl.BlockSpec((bm, DIFF_WIDTH), row),
            pl.BlockSpec((bm, d), row),
            pl.BlockSpec((bm, d), lambda i: (i, 1)),
            pl.BlockSpec((SB_WIDTH, d), const),
            pl.BlockSpec((DIFF_WIDTH, d), const),
            pl.BlockSpec((d, d), const),
            pl.BlockSpec((1, d), const),
            pl.BlockSpec((1, d), const),
        ],
        out_specs=pl.BlockSpec((bm, d), row),
        out_shape=jax.ShapeDtypeStruct((m, d), F32),
        compiler_params=pltpu.CompilerParams(
            dimension_semantics=("arbitrary",), vmem_limit_bytes=VMEM_LIMIT),
        name="merge",
    )(x, o_sb, o_d, gates, gates, w_sb, w_d, w_o, ln_g, ln_b)


def _ffn_kernel(x_ref, wu_ref, wd_ref, g_ref, b_ref, o_ref, xb_ref, acc_ref):
    f = pl.program_id(1)

    @pl.when(f == 0)
    def _():
        xb_ref[...] = x_ref[...].astype(BF16)
        acc_ref[...] = jnp.zeros_like(acc_ref)

    u = _dot(xb_ref[...], wu_ref[...])
    a = jnp.square(jnp.maximum(u, 0.0)).astype(BF16)
    acc_ref[...] += _dot(a, wd_ref[...])

    @pl.when(f == pl.num_programs(1) - 1)
    def _():
        o_ref[...] = _layer_norm(ALPHA * x_ref[...] + acc_ref[...], g_ref[...], b_ref[...])


def _ffn(x, w_up, w_down, ln_g, ln_b, *, bm, bf):
    m, d = x.shape
    dff = w_up.shape[1]
    return pl.pallas_call(
        _ffn_kernel,
        grid=(m // bm, dff // bf),
        in_specs=[
            pl.BlockSpec((bm, d), lambda i, f: (i, 0)),
            pl.BlockSpec((d, bf), lambda i, f: (0, f)),
            pl.BlockSpec((bf, d), lambda i, f: (f, 0)),
            pl.BlockSpec((1, d), lambda i, f: (0, 0)),
            pl.BlockSpec((1, d), lambda i, f: (0, 0)),
        ],
        out_specs=pl.BlockSpec((bm, d), lambda i, f: (i, 0)),
        out_shape=jax.ShapeDtypeStruct((m, d), F32),
        scratch_shapes=[pltpu.VMEM((bm, d), BF16), pltpu.VMEM((bm, d), F32)],
        compiler_params=pltpu.CompilerParams(
            dimension_semantics=("arbitrary", "arbitrary"), vmem_limit_bytes=VMEM_LIMIT),
        name="ffn",
    )(x, w_up, w_down, ln_g, ln_b)


N_ROWS = 8


def _sample_attn_kernel(pt_ref, lq1_ref, lk1_ref, lq2_ref, lk2_ref, g_ref, qsb_ref, qd_ref, kself_ref, vself_ref,
                        lt_ref, sbc_ref, dc_ref, osb_ref, od_ref, c_ref, asb_ref, m_ref, l_ref, ad_ref):
    del pt_ref
    p = pl.program_id(1)
    n = sbc_ref.shape[0] * N_ROWS
    n_col = n // LANES
    rows = lax.broadcasted_iota(jnp.int32, (N_ROWS, n), 0)
    lanes = lax.broadcasted_iota(jnp.int32, (N_ROWS, n), 1)
    own = (lanes % N_ROWS) == rows
    q_sb, q_d = qsb_ref[0], qd_ref[0]

    @pl.when(p == 0)
    def _():
        c_ref[...] = jnp.zeros_like(c_ref)
        asb_ref[...] = jnp.zeros_like(asb_ref)
        s_self = jnp.sum(q_d.astype(F32) * kself_ref[0].astype(F32), axis=-1, keepdims=True)
        m_ref[...] = jnp.broadcast_to(s_self, m_ref.shape)
        l_ref[...] = jnp.ones_like(l_ref)
        ad_ref[...] = vself_ref[0]

    k2 = sbc_ref[:, 0].reshape(n, HEAD_DIM).astype(BF16)
    v2 = sbc_ref[:, 1].reshape(n, HEAD_DIM).astype(BF16)
    g = _nt_dot(q_sb, k2)
    lk = jnp.where(own, -(jnp.maximum(g, 0.0) + jnp.log1p(jnp.exp(-jnp.abs(g)))), 0.0)
    lk_cols = jnp.concatenate([lk[:, j * LANES:(j + 1) * LANES] for j in range(n_col)], axis=0)
    s2 = _split_dot(lk_cols, lt_ref[...])
    c = c_ref[...]
    ws = [None] * n_col
    for j in reversed(range(n_col)):
        sl = slice(j * LANES, (j + 1) * LANES)
        e = g[:, sl] + lk[:, sl] + s2[j * N_ROWS:(j + 1) * N_ROWS, :LANES] + c
        ws[j] = jnp.where(own[:, sl], jnp.exp(e), 0.0)
        c = c + s2[j * N_ROWS:(j + 1) * N_ROWS, LANES:]
    c_ref[...] = c
    asb_ref[...] += _dot(jnp.concatenate(ws, axis=1).astype(BF16), v2)

    kd = dc_ref[:, 0].reshape(n, HEAD_DIM).astype(BF16)
    vd = dc_ref[:, 1].reshape(n, HEAD_DIM).astype(BF16)
    s = jnp.where(own, _nt_dot(q_d, kd), -jnp.inf)
    m = m_ref[...][:, :1]
    m_new = jnp.maximum(m, jnp.max(s, axis=-1, keepdims=True))
    a = jnp.exp(m - m_new)
    pr = jnp.exp(s - m_new)
    l_ref[...] = a * l_ref[...] + jnp.sum(pr, axis=-1, keepdims=True)
    half = N_ROWS // 2
    shifted = jnp.where(rows < half, pltpu.roll(pr, half, 1), pltpu.roll(pr, n - half, 1))
    lhs = jnp.concatenate([pr, shifted], axis=0).astype(BF16)
    ad_ref[...] = jnp.concatenate([a, a], axis=0) * ad_ref[...] + _dot(lhs, vd)
    m_ref[...] = jnp.broadcast_to(m_new, m_ref.shape)

    @pl.when(p == pl.num_programs(1) - 1)
    def _():
        osb_ref[0] = asb_ref[...].astype(osb_ref.dtype)
        lam = _lambda(lq1_ref, lk1_ref, lq2_ref, lk2_ref)
        l = l_ref[...][:, :1]
        x = ad_ref[...] / jnp.concatenate([l, l], axis=0)
        lo = x[0:half] - lam * x[3 * half:4 * half]
        hi = x[2 * half:3 * half] - lam * x[half:2 * half]
        ms = (jnp.sum(jnp.square(lo), axis=-1, keepdims=True)
              + jnp.sum(jnp.square(hi), axis=-1, keepdims=True)) / (2 * HEAD_DIM)
        r = lax.rsqrt(ms + LN_EPS)
        gain = g_ref[...]
        od_ref[0] = jnp.concatenate([((lo * r) * gain[:, :HEAD_DIM]) * (1.0 - LAM_INIT),
                                     ((hi * r) * gain[:, HEAD_DIM:]) * (1.0 - LAM_INIT)], axis=1).astype(od_ref.dtype)


def _column_suffix_matrix():
    i = jnp.arange(LANES)
    same = (i[:, None] % N_ROWS) == (i[None, :] % N_ROWS)
    later = (i[:, None] // N_ROWS) > (i[None, :] // N_ROWS)
    return jnp.concatenate([same & later, same], axis=1).astype(BF16)


def _sample_attention(page_table, qkv_s, cache_sb, cache_d, lams, diff_norm_g, *, page):
    b, n_pages = page_table.shape
    half = N_DIFF_HEADS

    def by_half(x):
        return x.reshape(b, N_DIFF_HEADS, 2, HEAD_DIM).transpose(0, 2, 1, 3).reshape(b, N_ROWS, HEAD_DIM)

    q_sb = qkv_s[:, COL_SBQ:COL_SBQ + SEG].reshape(b, N_ROWS, HEAD_DIM)
    q_d = by_half(qkv_s[:, COL_DQ:COL_DQ + SEG])
    k_self = by_half(qkv_s[:, COL_DK:COL_DK + SEG])
    v = by_half(qkv_s[:, COL_DV:COL_DV + SEG]).astype(F32)
    v_lo, v_hi = v[:, :half], v[:, half:]
    v_self = jnp.concatenate([v_lo, v_hi, v_hi, v_lo], axis=1)

    vec = pl.BlockSpec((1, HEAD_DIM), lambda i, p, pt: (0, 0))
    newest_first = lambda i, p, pt: (pt[i, n_pages - 1 - p], 0, 0, 0)
    per_seq = lambda i, p, pt: (i, 0, 0)
    grid_spec = pltpu.PrefetchScalarGridSpec(
        num_scalar_prefetch=1,
        grid=(b, n_pages),
        in_specs=[
            vec, vec, vec, vec,
            pl.BlockSpec((1, 2 * HEAD_DIM), lambda i, p, pt: (0, 0)),
            pl.BlockSpec((1, N_ROWS, HEAD_DIM), per_seq),
            pl.BlockSpec((1, N_ROWS, HEAD_DIM), per_seq),
            pl.BlockSpec((1, N_ROWS, HEAD_DIM), per_seq),
            pl.BlockSpec((1, 2 * N_ROWS, HEAD_DIM), per_seq),
            pl.BlockSpec((LANES, 2 * LANES), lambda i, p, pt: (0, 0)),
            pl.BlockSpec((page, 2, N_ROWS, HEAD_DIM), newest_first),
            pl.BlockSpec((page, 2, N_ROWS, HEAD_DIM), newest_first),
        ],
        out_specs=[
            pl.BlockSpec((1, N_SB_HEADS, HEAD_DIM), per_seq),
            pl.BlockSpec((1, N_DIFF_HEADS, 2 * HEAD_DIM), per_seq),
        ],
        scratch_shapes=[
            pltpu.VMEM((N_ROWS, LANES), F32),
            pltpu.VMEM((N_ROWS, HEAD_DIM), F32),
            pltpu.VMEM((N_ROWS, LANES), F32),
            pltpu.VMEM((N_ROWS, LANES), F32),
            pltpu.VMEM((2 * N_ROWS, HEAD_DIM), F32),
        ],
    )
    o_sb, o_d = pl.pallas_call(
        _sample_attn_kernel,
        grid_spec=grid_spec,
        out_shape=[
            jax.ShapeDtypeStruct((b, N_SB_HEADS, HEAD_DIM), BF16),
            jax.ShapeDtypeStruct((b, N_DIFF_HEADS, 2 * HEAD_DIM), BF16),
        ],
        compiler_params=pltpu.CompilerParams(
            dimension_semantics=("arbitrary", "arbitrary"), vmem_limit_bytes=VMEM_LIMIT),
        name="sample_attn",
    )(page_table, *lams, diff_norm_g, q_sb, q_d, k_self, v_self, _column_suffix_matrix(), cache_sb, cache_d)
    return o_sb.reshape(b, SB_WIDTH), o_d.reshape(b, DIFF_WIDTH)


def _suffix_matrix(n):
    j = jnp.arange(n)[:, None]
    s = jnp.arange(n)[None, :]
    return jnp.concatenate([(j > s), jnp.ones((n, n), bool)], axis=1).astype(BF16)


def _pick(m, candidates):
    for c in candidates:
        if m % c == 0:
            return c
    raise ValueError(f"no block size for {m}")


def kernel(x_prompt, x_sample, cache_sb_kv, cache_diff_kv, page_table, meta_tokens, w_in, lambda_q1, lambda_k1,
           lambda_q2, lambda_k2, diff_norm_g, w_branch_sb, w_branch_diff, w_out, ln_mix_g, ln_mix_b, w_up,
           w_down, ln_ffn_g, ln_ffn_b):
    assert w_in.shape[0] == DEPTH and x_prompt.shape[0] == 1 and x_sample.shape[1] == 1
    seq, d = x_prompt.shape[1:]
    n_dec = x_sample.shape[0]
    n_pool, page = cache_sb_kv.shape[1:3]
    past_len = page_table.shape[1] * page
    bq = 128

    w_in_b = w_in[0].astype(BF16)
    w_sb_b, w_d_b, w_o_b = w_branch_sb[0].astype(BF16), w_branch_diff[0].astype(BF16), w_out[0].astype(BF16)
    w_up_b, w_down_b = w_up[0].astype(BF16), w_down[0].astype(BF16)
    lams = (lambda_q1, lambda_k1, lambda_q2, lambda_k2)

    xp = x_prompt[0]
    rope_p = _rope_table(N_META + jnp.arange(seq))
    qkv_p, sbkv_p, dkv_p, gates_p = _project(xp, w_in_b, rope_p, bm=_pick(seq, (1024, 512, 256, 128)), bn=512)
    x_small = jnp.concatenate([meta_tokens.astype(F32), x_sample[:, 0]], axis=0)
    pos_small = jnp.concatenate([jnp.arange(N_META), jnp.full((n_dec,), past_len)])
    qkv_s, sbkv_s, dkv_s, gates_s = _project(x_small, w_in_b, _rope_table(pos_small), bm=N_META + n_dec, bn=512)

    tri = _suffix_matrix(bq)
    qkv_meta = jnp.concatenate([qkv_s[:N_META], jnp.zeros((bq - N_META, QKV_COLS), BF16)], axis=0)
    o_sb = _sb_attention(qkv_p, qkv_meta, tri, bq=bq)
    bqd = _pick(seq, (256, 128))
    o_d = _diff_attention(qkv_p, qkv_meta, lams, diff_norm_g, bq=bqd, bk=bqd)
    x1 = _merge(xp, o_sb, o_d, gates_p, w_sb_b, w_d_b, w_o_b, ln_mix_g, ln_mix_b, bm=_pick(seq, (256, 128)))
    y_prompt = _ffn(x1, w_up_b, w_down_b, ln_ffn_g, ln_ffn_b, bm=_pick(seq, (512, 256, 128)), bf=512)

    cache_sb = cache_sb_kv.reshape(n_pool * page, 2, N_SB_HEADS, HEAD_DIM)
    cache_d = cache_diff_kv.reshape(n_pool * page, 2, N_DIFF_HEADS, 2, HEAD_DIM).transpose(0, 1, 3, 2, 4).reshape(
        n_pool * page, 2, N_ROWS, HEAD_DIM)
    os_sb, os_d = _sample_attention(page_table, qkv_s[N_META:], cache_sb, cache_d, lams, diff_norm_g, page=page)
    xs1 = _merge(x_sample[:, 0], os_sb, os_d, gates_s[N_META:], w_sb_b, w_d_b, w_o_b, ln_mix_g, ln_mix_b,
                 bm=n_dec)
    y_sample = _ffn(xs1, w_up_b, w_down_b, ln_ffn_g, ln_ffn_b, bm=n_dec, bf=512)

    t_p = N_META + seq
    sb_kv_prompt = jnp.concatenate([sbkv_s[:N_META], sbkv_p], axis=0).reshape(1, 1, t_p, 2, N_SB_HEADS, HEAD_DIM)
    diff_kv_prompt = jnp.concatenate([dkv_s[:N_META], dkv_p], axis=0).reshape(
        1, 1, t_p, 2, N_DIFF_HEADS, 2 * HEAD_DIM)
    sb_kv_sample = sbkv_s[N_META:].reshape(1, n_dec, 1, 2, N_SB_HEADS, HEAD_DIM)
    diff_kv_sample = dkv_s[N_META:].reshape(1, n_dec, 1, 2, N_DIFF_HEADS, 2 * HEAD_DIM)
    return (y_prompt[None], y_sample[:, None], sb_kv_prompt, diff_kv_prompt, sb_kv_sample, diff_kv_sample)
```

```python
import functools
import math

import jax
import jax.numpy as jnp
from jax import lax
from jax.experimental import pallas as pl
from jax.experimental.pallas import tpu as pltpu

F32 = jnp.float32
BF16 = jnp.bfloat16

HEAD_DIM = 128
N_SB_HEADS = 8
N_DIFF_HEADS = 4
SB_WIDTH = N_SB_HEADS * HEAD_DIM
DIFF_WIDTH = N_DIFF_HEADS * 2 * HEAD_DIM
N_META = 16
ROT_DIM = HEAD_DIM // 4
ROPE_THETA = 500000.0
LN_EPS = 1e-5
DEPTH = 1
ALPHA = (2 * DEPTH) ** 0.25
LAM_INIT = 0.8 - 0.6 * math.exp(-0.3 * 0)
QK_SCALE = HEAD_DIM ** -0.5
LANES = 128
VMEM_LIMIT = 56 * 1024 * 1024

COL_SBQ, COL_SBK, COL_SBV = 0, SB_WIDTH, 2 * SB_WIDTH
COL_DQ, COL_DK, COL_DV = 3 * SB_WIDTH, 3 * SB_WIDTH + DIFF_WIDTH, 3 * SB_WIDTH + 2 * DIFF_WIDTH
QKV_COLS = 3 * SB_WIDTH + 3 * DIFF_WIDTH
SEG = 1024


def _nt_dot(a, b):
    return lax.dot_general(a, b, (((1,), (1,)), ((), ())), preferred_element_type=F32)


def _dot(a, b):
    return jnp.dot(a, b, preferred_element_type=F32)


def _rope_table(pos):
    half = ROT_DIM // 2
    inv = ROPE_THETA ** (-jnp.arange(half, dtype=F32) / half)
    ang = pos.astype(F32)[:, None] * inv[None, :]
    cos, sin = jnp.cos(ang), jnp.sin(ang)
    t = pos.shape[0]
    rest1 = jnp.ones((t, HEAD_DIM - ROT_DIM), F32)
    rest0 = jnp.zeros((t, HEAD_DIM - ROT_DIM), F32)
    z = jnp.zeros((t, half), F32)
    c = jnp.concatenate([cos, cos, rest1], axis=1)
    s1 = jnp.concatenate([-sin, z, rest0], axis=1)
    s2 = jnp.concatenate([z, sin, rest0], axis=1)
    return jnp.concatenate([c, s1, s2], axis=1)


def _rope(z, rope):
    c, s1, s2 = rope[:, :LANES], rope[:, LANES:2 * LANES], rope[:, 2 * LANES:]
    half = ROT_DIM // 2
    outs = []
    for i in range(z.shape[1] // LANES):
        zc = z[:, i * LANES:(i + 1) * LANES]
        outs.append(zc * c + pltpu.roll(zc, LANES - half, 1) * s1 + pltpu.roll(zc, half, 1) * s2)
    return jnp.concatenate(outs, axis=1)


def _proj_kernel(x_ref, w_ref, rope_ref, qkv_ref, sbkv_ref, dkv_ref, gate_ref, xb_ref, *, nb):
    j = pl.program_id(1)

    @pl.when(j == 0)
    def _():
        xb_ref[...] = x_ref[...].astype(BF16)

    z = _dot(xb_ref[...], w_ref[...])
    seg = j // nb

    @pl.when(seg == 0)
    def _():
        qkv_ref[...] = (z * QK_SCALE).astype(BF16)

    @pl.when((seg == 1) | (seg == 2))
    def _():
        qkv_ref[...] = z.astype(BF16)
        sbkv_ref[...] = z

    @pl.when(seg == 3)
    def _():
        qkv_ref[...] = (_rope(z, rope_ref[...]) * QK_SCALE).astype(BF16)

    @pl.when(seg == 4)
    def _():
        r = _rope(z, rope_ref[...])
        qkv_ref[...] = r.astype(BF16)
        dkv_ref[...] = r

    @pl.when(seg == 5)
    def _():
        qkv_ref[...] = z.astype(BF16)
        dkv_ref[...] = z

    @pl.when(seg >= 6)
    def _():
        gate_ref[...] = jax.nn.sigmoid(z).astype(BF16)


def _project(x, w_bf16, rope, *, bm, bn):
    m, d = x.shape
    n = w_bf16.shape[1]
    nb = SEG // bn
    grid = (m // bm, n // bn)
    return pl.pallas_call(
        functools.partial(_proj_kernel, nb=nb),
        grid=grid,
        in_specs=[
            pl.BlockSpec((bm, d), lambda i, j: (i, 0)),
            pl.BlockSpec((d, bn), lambda i, j: (0, j)),
            pl.BlockSpec((bm, 3 * LANES), lambda i, j: (i, 0)),
        ],
        out_specs=[
            pl.BlockSpec((bm, bn), lambda i, j: (i, jnp.minimum(j, 6 * nb - 1))),
            pl.BlockSpec((bm, bn), lambda i, j: (i, jnp.clip(j - nb, 0, 2 * nb - 1))),
            pl.BlockSpec((bm, bn), lambda i, j: (i, jnp.clip(j - 4 * nb, 0, 2 * nb - 1))),
            pl.BlockSpec((bm, bn), lambda i, j: (i, jnp.clip(j - 6 * nb, 0, 4 * nb - 1))),
        ],
        out_shape=[
            jax.ShapeDtypeStruct((m, QKV_COLS), BF16),
            jax.ShapeDtypeStruct((m, 2 * SB_WIDTH), F32),
            jax.ShapeDtypeStruct((m, 2 * DIFF_WIDTH), F32),
            jax.ShapeDtypeStruct((m, n - QKV_COLS), BF16),
        ],
        scratch_shapes=[pltpu.VMEM((bm, d), BF16)],
        compiler_params=pltpu.CompilerParams(
            dimension_semantics=("arbitrary", "arbitrary"), vmem_limit_bytes=VMEM_LIMIT),
        name="proj",
    )(x, w_bf16, rope)


def _split_dot(x, t):
    hi = x.astype(BF16)
    r1 = x - hi.astype(F32)
    mid = r1.astype(BF16)
    lo = (r1 - mid.astype(F32)).astype(BF16)
    return _dot(hi, t) + _dot(mid, t) + _dot(lo, t)


LOG_F32_ZERO = -104.0


def _sb_attn_kernel(q_ref, k_ref, v_ref, km_ref, vm_ref, tri_ref, o_ref, c_ref, acc_ref, *, bq, n_heads):
    qb = pl.program_id(1)
    tri = tri_ref[...]
    row = lax.broadcasted_iota(jnp.int32, (n_heads * bq, bq), 0) % bq
    col = lax.broadcasted_iota(jnp.int32, (n_heads * bq, bq), 1)
    head_cols = [slice(g * HEAD_DIM, (g + 1) * HEAD_DIM) for g in range(n_heads)]
    qs = [q_ref[:, hc] for hc in head_cols]

    def block(kv_rows, k_ref, v_ref, mask, c, acc):
        z = jnp.concatenate([_nt_dot(q, k_ref[kv_rows, hc]) for q, hc in zip(qs, head_cols)], axis=0)
        lk = -(jnp.maximum(z, 0.0) + jnp.log1p(jnp.exp(-jnp.abs(z))))
        if mask is not None:
            lk = jnp.where(mask, lk, 0.0)
        s2 = _split_dot(lk, tri)
        w = jnp.exp(z + lk + s2[:, :bq] + c)
        if mask is not None:
            w = jnp.where(mask, w, 0.0)
        w = w.astype(BF16)
        pv = [_dot(w[g * bq:(g + 1) * bq], v_ref[kv_rows, hc]) for g, hc in enumerate(head_cols)]
        return c + s2[:, bq:], acc + jnp.concatenate(pv, axis=0)

    start = pl.multiple_of(qb * bq, bq)
    c, acc = block(pl.ds(start, bq), k_ref, v_ref, col < row,
                   jnp.zeros((n_heads * bq, bq), F32), jnp.zeros((n_heads * bq, HEAD_DIM), F32))
    c_ref[...] = c
    acc_ref[...] = acc

    def live(carry):
        kb, c_max = carry
        return (kb >= 0) & (c_max > LOG_F32_ZERO)

    def body(carry):
        kb, _ = carry
        s = pl.multiple_of(kb * bq, bq)
        c, acc = block(pl.ds(s, bq), k_ref, v_ref, None, c_ref[...], acc_ref[...])
        c_ref[...] = c
        acc_ref[...] = acc
        return kb - 1, jnp.max(c)

    _, c_max = lax.while_loop(live, body, (qb - 1, jnp.max(c)))

    @pl.when(c_max > LOG_F32_ZERO)
    def _():
        _, acc = block(slice(None), km_ref, vm_ref, col < N_META, c_ref[...], acc_ref[...])
        acc_ref[...] = acc

    for g, hc in enumerate(head_cols):
        o_ref[:, hc] = acc_ref[g * bq:(g + 1) * bq, :].astype(o_ref.dtype)


def _sb_attention(qkv, qkv_meta, tri, *, bq, n_heads):
    t = qkv.shape[0]
    w = n_heads * HEAD_DIM
    kb, vb = COL_SBK // w, COL_SBV // w
    return pl.pallas_call(
        functools.partial(_sb_attn_kernel, bq=bq, n_heads=n_heads),
        grid=(N_SB_HEADS // n_heads, t // bq),
        in_specs=[
            pl.BlockSpec((bq, w), lambda h, i: (i, h)),
            pl.BlockSpec((t, w), lambda h, i: (0, kb + h)),
            pl.BlockSpec((t, w), lambda h, i: (0, vb + h)),
            pl.BlockSpec((bq, w), lambda h, i: (0, kb + h)),
            pl.BlockSpec((bq, w), lambda h, i: (0, vb + h)),
            pl.BlockSpec((bq, 2 * bq), lambda h, i: (0, 0)),
        ],
        out_specs=pl.BlockSpec((bq, w), lambda h, i: (i, h)),
        out_shape=jax.ShapeDtypeStruct((t, SB_WIDTH), BF16),
        scratch_shapes=[pltpu.VMEM((n_heads * bq, bq), F32), pltpu.VMEM((n_heads * bq, HEAD_DIM), F32)],
        compiler_params=pltpu.CompilerParams(
            dimension_semantics=("arbitrary", "arbitrary"), vmem_limit_bytes=VMEM_LIMIT),
        name="sb_attn",
    )(qkv, qkv, qkv, qkv_meta, qkv_meta, tri)


def _lambda(lq1_ref, lk1_ref, lq2_ref, lk2_ref):
    a = jnp.sum(lq1_ref[...] * lk1_ref[...], axis=-1, keepdims=True)
    b = jnp.sum(lq2_ref[...] * lk2_ref[...], axis=-1, keepdims=True)
    return jnp.exp(a) - jnp.exp(b) + LAM_INIT


def _head_norm(o, g):
    of = o * lax.rsqrt(jnp.mean(jnp.square(o), axis=-1, keepdims=True) + LN_EPS)
    return (of * g) * (1.0 - LAM_INIT)


def _diff_attn_kernel(lq1_ref, lk1_ref, lq2_ref, lk2_ref, g_ref, q1_ref, q2_ref, k1_ref, k2_ref, v_ref,
                      km1_ref, km2_ref, vm_ref, o_ref, *, bq, bk):
    qb = pl.program_id(1)
    lam = _lambda(lq1_ref, lk1_ref, lq2_ref, lk2_ref)
    q1, q2 = q1_ref[...], q2_ref[...]
    n_meta_rows = km1_ref.shape[0]
    n_full = qb * (bq // bk)
    meta_mask = lax.broadcasted_iota(jnp.int32, (bq, n_meta_rows), 1) < N_META
    row = lax.broadcasted_iota(jnp.int32, (bq, bk), 0)
    col = lax.broadcasted_iota(jnp.int32, (bq, bk), 1)

    def scores(k1, k2, mask):
        s1, s2 = _nt_dot(q1, k1), _nt_dot(q2, k2)
        if mask is not None:
            s1, s2 = jnp.where(mask, s1, -jnp.inf), jnp.where(mask, s2, -jnp.inf)
        return s1, s2

    def lanewise(op, x):
        out = x[:, :LANES]
        for j in range(1, x.shape[1] // LANES):
            out = op(out, x[:, j * LANES:(j + 1) * LANES])
        return out

    def key_block(start):
        s = pl.multiple_of(start, bk)
        return k1_ref[pl.ds(s, bk), :], k2_ref[pl.ds(s, bk), :], v_ref[pl.ds(s, bk), :]

    def diag_blocks():
        for j in range(bq // bk):
            yield key_block(qb * bq + j * bk) + (col + j * bk <= row,)

    def max_update(mx, k1, k2, mask):
        s1, s2 = scores(k1, k2, mask)
        return jnp.maximum(mx[0], lanewise(jnp.maximum, s1)), jnp.maximum(mx[1], lanewise(jnp.maximum, s2))

    neg = jnp.full((bq, LANES), -jnp.inf, F32)
    mx = max_update((neg, neg), km1_ref[...], km2_ref[...], meta_mask)
    mx = lax.fori_loop(0, n_full, lambda i, mx: max_update(mx, *key_block(i * bk)[:2], None), mx)
    for k1, k2, _, mask in diag_blocks():
        mx = max_update(mx, k1, k2, mask)
    m1 = jnp.max(mx[0], axis=-1, keepdims=True)
    m2 = jnp.max(mx[1], axis=-1, keepdims=True)

    def acc_update(st, k1, k2, v, mask):
        s1, s2 = scores(k1, k2, mask)
        p1, p2 = jnp.exp(s1 - m1), jnp.exp(s2 - m2)
        return (st[0] + lanewise(jnp.add, p1), st[1] + _dot(p1.astype(BF16), v),
                st[2] + lanewise(jnp.add, p2), st[3] + _dot(p2.astype(BF16), v))

    zl, za = jnp.zeros((bq, LANES), F32), jnp.zeros((bq, 2 * HEAD_DIM), F32)
    st = acc_update((zl, za, zl, za), km1_ref[...], km2_ref[...], vm_ref[...], meta_mask)
    st = lax.fori_loop(0, n_full, lambda i, st: acc_update(st, *key_block(i * bk), None), st)
    for k1, k2, v, mask in diag_blocks():
        st = acc_update(st, k1, k2, v, mask)
    l1 = jnp.sum(st[0], axis=-1, keepdims=True)
    l2 = jnp.sum(st[2], axis=-1, keepdims=True)
    o = st[1] / l1 - lam * (st[3] / l2)
    o_ref[...] = _head_norm(o, g_ref[...]).astype(o_ref.dtype)


def _diff_attention(qkv, qkv_meta, lams, diff_norm_g, *, bq, bk):
    t = qkv.shape[0]
    mrows = qkv_meta.shape[0]
    qb, kb, vb = COL_DQ // HEAD_DIM, COL_DK // HEAD_DIM, COL_DV // (2 * HEAD_DIM)
    vec = pl.BlockSpec((1, HEAD_DIM), lambda h, i: (0, 0))
    return pl.pallas_call(
        functools.partial(_diff_attn_kernel, bq=bq, bk=bk),
        grid=(N_DIFF_HEADS, t // bq),
        in_specs=[
            vec, vec, vec, vec,
            pl.BlockSpec((1, 2 * HEAD_DIM), lambda h, i: (0, 0)),
            pl.BlockSpec((bq, HEAD_DIM), lambda h, i: (i, qb + 2 * h)),
            pl.BlockSpec((bq, HEAD_DIM), lambda h, i: (i, qb + 2 * h + 1)),
            pl.BlockSpec((t, HEAD_DIM), lambda h, i: (0, kb + 2 * h)),
            pl.BlockSpec((t, HEAD_DIM), lambda h, i: (0, kb + 2 * h + 1)),
            pl.BlockSpec((t, 2 * HEAD_DIM), lambda h, i: (0, vb + h)),
            pl.BlockSpec((mrows, HEAD_DIM), lambda h, i: (0, kb + 2 * h)),
            pl.BlockSpec((mrows, HEAD_DIM), lambda h, i: (0, kb + 2 * h + 1)),
            pl.BlockSpec((mrows, 2 * HEAD_DIM), lambda h, i: (0, vb + h)),
        ],
        out_specs=pl.BlockSpec((bq, 2 * HEAD_DIM), lambda h, i: (i, h)),
        out_shape=jax.ShapeDtypeStruct((t, DIFF_WIDTH), BF16),
        compiler_params=pltpu.CompilerParams(
            dimension_semantics=("arbitrary", "arbitrary"), vmem_limit_bytes=VMEM_LIMIT),
        name="diff_attn",
    )(*lams, diff_norm_g, qkv, qkv, qkv, qkv, qkv, qkv_meta, qkv_meta, qkv_meta)


def _layer_norm(x, g, b):
    mu = jnp.mean(x, axis=-1, keepdims=True)
    xc = x - mu
    var = jnp.mean(jnp.square(xc), axis=-1, keepdims=True)
    return (xc * lax.rsqrt(var + LN_EPS)) * g + b


def _merge_kernel(x_ref, osb_ref, od_ref, gsb_ref, gd_ref, wsb_ref, wd_ref, wo_ref, g_ref, b_ref, o_ref):
    br_sb = _dot(osb_ref[...], wsb_ref[...])
    br_d = _dot(od_ref[...], wd_ref[...])
    gated = gsb_ref[...].astype(F32) * br_sb + gd_ref[...].astype(F32) * br_d
    mix = _dot(gated.astype(BF16), wo_ref[...])
    o_ref[...] = _layer_norm(ALPHA * x_ref[...] + mix, g_ref[...], b_ref[...])


def _merge(x, o_sb, o_d, gates, w_sb, w_d, w_o, ln_g, ln_b, *, bm):
    m, d = x.shape
    row = lambda i: (i, 0)
    const = lambda i: (0, 0)
    return pl.pallas_call(
        _merge_kernel,
        grid=(m // bm,),
        in_specs=[
            pl.BlockSpec((bm, d), row),
            pl.BlockSpec((bm, SB_WIDTH), row),
            pl.BlockSpec((bm, DIFF_WIDTH), row),
            pl.BlockSpec((bm, d), row),
            pl.BlockSpec((bm, d), lambda i: (i, 1)),
            pl.BlockSpec((SB_WIDTH, d), const),
            pl.BlockSpec((DIFF_WIDTH, d), const),
            pl.BlockSpec((d, d), const),
            pl.BlockSpec((1, d), const),
            pl.BlockSpec((1, d), const),
        ],
        out_specs=pl.BlockSpec((bm, d), row),
        out_shape=jax.ShapeDtypeStruct((m, d), F32),
        compiler_params=pltpu.CompilerParams(
            dimension_semantics=("arbitrary",), vmem_limit_bytes=VMEM_LIMIT),
        name="merge",
    )(x, o_sb, o_d, gates, gates, w_sb, w_d, w_o, ln_g, ln_b)


def _ffn_kernel(x_ref, wu_ref, wd_ref, g_ref, b_ref, o_ref, xb_ref, acc_ref):
    f = pl.program_id(1)

    @pl.when(f == 0)
    def _():
        xb_ref[...] = x_ref[...].astype(BF16)
        acc_ref[...] = jnp.zeros_like(acc_ref)

    u = _dot(xb_ref[...], wu_ref[...])
    a = jnp.square(jnp.maximum(u, 0.0)).astype(BF16)
    acc_ref[...] += _dot(a, wd_ref[...])

    @pl.when(f == pl.num_programs(1) - 1)
    def _():
        o_ref[...] = _layer_norm(ALPHA * x_ref[...] + acc_ref[...], g_ref[...], b_ref[...])


def _ffn(x, w_up, w_down, ln_g, ln_b, *, bm, bf):
    m, d = x.shape
    dff = w_up.shape[1]
    return pl.pallas_call(
        _ffn_kernel,
        grid=(m // bm, dff // bf),
        in_specs=[
            pl.BlockSpec((bm, d), lambda i, f: (i, 0)),
            pl.BlockSpec((d, bf), lambda i, f: (0, f)),
            pl.BlockSpec((bf, d), lambda i, f: (f, 0)),
            pl.BlockSpec((1, d), lambda i, f: (0, 0)),
            pl.BlockSpec((1, d), lambda i, f: (0, 0)),
        ],
        out_specs=pl.BlockSpec((bm, d), lambda i, f: (i, 0)),
        out_shape=jax.ShapeDtypeStruct((m, d), F32),
        scratch_shapes=[pltpu.VMEM((bm, d), BF16), pltpu.VMEM((bm, d), F32)],
        compiler_params=pltpu.CompilerParams(
            dimension_semantics=("arbitrary", "arbitrary"), vmem_limit_bytes=VMEM_LIMIT),
        name="ffn",
    )(x, w_up, w_down, ln_g, ln_b)


N_ROWS = 8


def _sample_attn_kernel(pt_ref, lq1_ref, lk1_ref, lq2_ref, lk2_ref, g_ref, qsb_ref, qd_ref, kself_ref, vself_ref,
                        lt_ref, *refs, n_seq):
    del pt_ref
    sbc_refs, dc_refs = refs[:n_seq], refs[n_seq:2 * n_seq]
    osb_ref, od_ref, c_ref, asb_ref, m_ref, l_ref, ad_ref = refs[2 * n_seq:]
    p = pl.program_id(1)
    n = sbc_refs[0].shape[0] * N_ROWS
    n_col = n // LANES
    n_row = n_seq * N_ROWS
    half = N_ROWS // 2
    rows = lax.broadcasted_iota(jnp.int32, (n_row, n), 0) % N_ROWS
    lanes = lax.broadcasted_iota(jnp.int32, (n_row, n), 1)
    own = (lanes % N_ROWS) == rows

    def page_matrix(ref, kv):
        return ref[:, kv].reshape(n, HEAD_DIM).astype(BF16)

    @pl.when(p == 0)
    def _():
        c_ref[...] = jnp.zeros_like(c_ref)
        asb_ref[...] = jnp.zeros_like(asb_ref)
        l_ref[...] = jnp.ones_like(l_ref)
        ad_ref[...] = vself_ref[...]
        qk = qd_ref[...].astype(F32) * kself_ref[...].astype(F32)
        s_self = jnp.sum(qk.reshape(n_row, HEAD_DIM), axis=-1, keepdims=True)
        m_ref[...] = jnp.broadcast_to(s_self, m_ref.shape)

    @pl.when(jnp.max(c_ref[...]) > LOG_F32_ZERO)
    def _():
        g = jnp.concatenate([_nt_dot(qsb_ref[u], page_matrix(sbc_refs[u], 0)) for u in range(n_seq)], axis=0)
        lk = jnp.where(own, -(jnp.maximum(g, 0.0) + jnp.log1p(jnp.exp(-jnp.abs(g)))), 0.0)
        lk_cols = jnp.concatenate([lk[:, j * LANES:(j + 1) * LANES] for j in range(n_col)], axis=0)
        s2 = _split_dot(lk_cols, lt_ref[...])
        c = c_ref[...]
        ws = [None] * n_col
        for j in reversed(range(n_col)):
            sl = slice(j * LANES, (j + 1) * LANES)
            e = g[:, sl] + lk[:, sl] + s2[j * n_row:(j + 1) * n_row, :LANES] + c
            ws[j] = jnp.where(own[:, sl], jnp.exp(e), 0.0)
            c = c + s2[j * n_row:(j + 1) * n_row, LANES:]
        c_ref[...] = jnp.where(own[:, :LANES], c, -jnp.inf)
        w = jnp.concatenate(ws, axis=1).astype(BF16)
        for u in range(n_seq):
            asb_ref[u] += _dot(w[u * N_ROWS:(u + 1) * N_ROWS], page_matrix(sbc_refs[u], 1))

    s = jnp.concatenate([_nt_dot(qd_ref[u], page_matrix(dc_refs[u], 0)) for u in range(n_seq)], axis=0)
    s = jnp.where(own, s, -jnp.inf)
    m = m_ref[...][:, :1]
    m_new = jnp.maximum(m, jnp.max(s, axis=-1, keepdims=True))
    a = jnp.exp(m - m_new)
    pr = jnp.exp(s - m_new)
    l_ref[...] = a * l_ref[...] + jnp.sum(pr, axis=-1, keepdims=True)
    m_ref[...] = jnp.broadcast_to(m_new, m_ref.shape)
    s_other = jnp.where(rows < half, pltpu.roll(s, half, 1), pltpu.roll(s, n - half, 1))
    pr_other = jnp.exp(s_other - m_new)
    for u in range(n_seq):
        sl = slice(u * N_ROWS, (u + 1) * N_ROWS)
        lhs = jnp.concatenate([pr[sl], pr_other[sl]], axis=0).astype(BF16)
        ad_ref[u] = jnp.concatenate([a[sl], a[sl]], axis=0) * ad_ref[u] + _dot(lhs, page_matrix(dc_refs[u], 1))

    @pl.when(p == pl.num_programs(1) - 1)
    def _():
        osb_ref[...] = asb_ref[...].astype(osb_ref.dtype)
        lam = _lambda(lq1_ref, lk1_ref, lq2_ref, lk2_ref)
        gain = g_ref[...]
        for u in range(n_seq):
            l = l_ref[u * N_ROWS:(u + 1) * N_ROWS, :1]
            x = ad_ref[u] / jnp.concatenate([l, l], axis=0)
            lo = x[0:half] - lam * x[3 * half:4 * half]
            hi = x[2 * half:3 * half] - lam * x[half:2 * half]
            ms = (jnp.sum(jnp.square(lo), axis=-1, keepdims=True)
                  + jnp.sum(jnp.square(hi), axis=-1, keepdims=True)) / (2 * HEAD_DIM)
            r = lax.rsqrt(ms + LN_EPS)
            od_ref[u] = jnp.concatenate([((lo * r) * gain[:, :HEAD_DIM]) * (1.0 - LAM_INIT),
                                         ((hi * r) * gain[:, HEAD_DIM:]) * (1.0 - LAM_INIT)],
                                        axis=1).astype(od_ref.dtype)


def _column_suffix_matrix():
    i = jnp.arange(LANES)
    same = (i[:, None] % N_ROWS) == (i[None, :] % N_ROWS)
    later = (i[:, None] // N_ROWS) > (i[None, :] // N_ROWS)
    return jnp.concatenate([same & later, same], axis=1).astype(BF16)


def _sample_attention(page_table, qkv_s, cache_sb, cache_d, lams, diff_norm_g, *, page, n_seq):
    b, n_pages = page_table.shape
    half = N_DIFF_HEADS

    def by_half(x):
        return x.reshape(b, N_DIFF_HEADS, 2, HEAD_DIM).transpose(0, 2, 1, 3).reshape(b, N_ROWS, HEAD_DIM)

    q_sb = qkv_s[:, COL_SBQ:COL_SBQ + SEG].reshape(b, N_ROWS, HEAD_DIM)
    q_d = by_half(qkv_s[:, COL_DQ:COL_DQ + SEG])
    k_self = by_half(qkv_s[:, COL_DK:COL_DK + SEG])
    v = by_half(qkv_s[:, COL_DV:COL_DV + SEG]).astype(F32)
    v_lo, v_hi = v[:, :half], v[:, half:]
    v_self = jnp.concatenate([v_lo, v_hi, v_hi, v_lo], axis=1)

    vec = pl.BlockSpec((1, HEAD_DIM), lambda i, p, pt: (0, 0))
    per_seq = lambda i, p, pt: (i, 0, 0)

    def page_spec(u):
        return pl.BlockSpec((page, 2, N_ROWS, HEAD_DIM),
                            lambda i, p, pt: (pt[i * n_seq + u, n_pages - 1 - p], 0, 0, 0))

    grid_spec = pltpu.PrefetchScalarGridSpec(
        num_scalar_prefetch=1,
        grid=(b // n_seq, n_pages),
        in_specs=[
            vec, vec, vec, vec,
            pl.BlockSpec((1, 2 * HEAD_DIM), lambda i, p, pt: (0, 0)),
            pl.BlockSpec((n_seq, N_ROWS, HEAD_DIM), per_seq),
            pl.BlockSpec((n_seq, N_ROWS, HEAD_DIM), per_seq),
            pl.BlockSpec((n_seq, N_ROWS, HEAD_DIM), per_seq),
            pl.BlockSpec((n_seq, 2 * N_ROWS, HEAD_DIM), per_seq),
            pl.BlockSpec((LANES, 2 * LANES), lambda i, p, pt: (0, 0)),
        ] + [page_spec(u) for u in range(n_seq)] * 2,
        out_specs=[
            pl.BlockSpec((n_seq, N_SB_HEADS, HEAD_DIM), per_seq),
            pl.BlockSpec((n_seq, N_DIFF_HEADS, 2 * HEAD_DIM), per_seq),
        ],
        scratch_shapes=[
            pltpu.VMEM((n_seq * N_ROWS, LANES), F32),
            pltpu.VMEM((n_seq, N_ROWS, HEAD_DIM), F32),
            pltpu.VMEM((n_seq * N_ROWS, LANES), F32),
            pltpu.VMEM((n_seq * N_ROWS, LANES), F32),
            pltpu.VMEM((n_seq, 2 * N_ROWS, HEAD_DIM), F32),
        ],
    )
    o_sb, o_d = pl.pallas_call(
        functools.partial(_sample_attn_kernel, n_seq=n_seq),
        grid_spec=grid_spec,
        out_shape=[
            jax.ShapeDtypeStruct((b, N_SB_HEADS, HEAD_DIM), BF16),
            jax.ShapeDtypeStruct((b, N_DIFF_HEADS, 2 * HEAD_DIM), BF16),
        ],
        compiler_params=pltpu.CompilerParams(
            dimension_semantics=("arbitrary", "arbitrary"), vmem_limit_bytes=VMEM_LIMIT),
        name="sample_attn",
    )(page_table, *lams, diff_norm_g, q_sb, q_d, k_self, v_self, _column_suffix_matrix(),
      *([cache_sb] * n_seq), *([cache_d] * n_seq))
    return o_sb.reshape(b, SB_WIDTH), o_d.reshape(b, DIFF_WIDTH)


def _suffix_matrix(n):
    j = jnp.arange(n)[:, None]
    s = jnp.arange(n)[None, :]
    return jnp.concatenate([(j > s), jnp.ones((n, n), bool)], axis=1).astype(BF16)


def _pick(m, candidates):
    for c in candidates:
        if m % c == 0:
            return c
    raise ValueError(f"no block size for {m}")


def kernel(x_prompt, x_sample, cache_sb_kv, cache_diff_kv, page_table, meta_tokens, w_in, lambda_q1, lambda_k1,
           lambda_q2, lambda_k2, diff_norm_g, w_branch_sb, w_branch_diff, w_out, ln_mix_g, ln_mix_b, w_up,
           w_down, ln_ffn_g, ln_ffn_b):
    assert w_in.shape[0] == DEPTH and x_prompt.shape[0] == 1 and x_sample.shape[1] == 1
    seq, d = x_prompt.shape[1:]
    n_dec = x_sample.shape[0]
    n_pool, page = cache_sb_kv.shape[1:3]
    past_len = page_table.shape[1] * page
    bq = 128

    w_in_b = w_in[0].astype(BF16)
    w_sb_b, w_d_b, w_o_b = w_branch_sb[0].astype(BF16), w_branch_diff[0].astype(BF16), w_out[0].astype(BF16)
    w_up_b, w_down_b = w_up[0].astype(BF16), w_down[0].astype(BF16)
    lams = (lambda_q1, lambda_k1, lambda_q2, lambda_k2)

    xp = x_prompt[0]
    rope_p = _rope_table(N_META + jnp.arange(seq))
    qkv_p, sbkv_p, dkv_p, gates_p = _project(xp, w_in_b, rope_p, bm=_pick(seq, (1024, 512, 256, 128)), bn=512)
    x_small = jnp.concatenate([meta_tokens.astype(F32), x_sample[:, 0]], axis=0)
    pos_small = jnp.concatenate([jnp.arange(N_META), jnp.full((n_dec,), past_len)])
    qkv_s, sbkv_s, dkv_s, gates_s = _project(x_small, w_in_b, _rope_table(pos_small), bm=N_META + n_dec, bn=512)

    tri = _suffix_matrix(bq)
    qkv_meta = jnp.concatenate([qkv_s[:N_META], jnp.zeros((bq - N_META, QKV_COLS), BF16)], axis=0)
    o_sb = _sb_attention(qkv_p, qkv_meta, tri, bq=bq, n_heads=4)
    bqd = _pick(seq, (512, 256, 128))
    o_d = _diff_attention(qkv_p, qkv_meta, lams, diff_norm_g, bq=bqd, bk=bqd)
    x1 = _merge(xp, o_sb, o_d, gates_p, w_sb_b, w_d_b, w_o_b, ln_mix_g, ln_mix_b, bm=_pick(seq, (256, 128)))
    y_prompt = _ffn(x1, w_up_b, w_down_b, ln_ffn_g, ln_ffn_b, bm=_pick(seq, (512, 256, 128)), bf=512)

    cache_sb = cache_sb_kv.reshape(n_pool * page, 2, N_SB_HEADS, HEAD_DIM)
    cache_d = cache_diff_kv.reshape(n_pool * page, 2, N_DIFF_HEADS, 2, HEAD_DIM).transpose(0, 1, 3, 2, 4).reshape(
        n_pool * page, 2, N_ROWS, HEAD_DIM)
    os_sb, os_d = _sample_attention(page_table, qkv_s[N_META:], cache_sb, cache_d, lams, diff_norm_g, page=page,
                                    n_seq=_pick(n_dec, (4, 2, 1)))
    xs1 = _merge(x_sample[:, 0], os_sb, os_d, gates_s[N_META:], w_sb_b, w_d_b, w_o_b, ln_mix_g, ln_mix_b,
                 bm=n_dec)
    y_sample = _ffn(xs1, w_up_b, w_down_b, ln_ffn_g, ln_ffn_b, bm=n_dec, bf=512)

    t_p = N_META + seq
    sb_kv_prompt = jnp.concatenate([sbkv_s[:N_META], sbkv_p], axis=0).reshape(1, 1, t_p, 2, N_SB_HEADS, HEAD_DIM)
    diff_kv_prompt = jnp.concatenate([dkv_s[:N_META], dkv_p], axis=0).reshape(
        1, 1, t_p, 2, N_DIFF_HEADS, 2 * HEAD_DIM)
    sb_kv_sample = sbkv_s[N_META:].reshape(1, n_dec, 1, 2, N_SB_HEADS, HEAD_DIM)
    diff_kv_sample = dkv_s[N_META:].reshape(1, n_dec, 1, 2, N_DIFF_HEADS, 2 * HEAD_DIM)
    return (y_prompt[None], y_sample[:, None], sb_kv_prompt, diff_kv_prompt, sb_kv_sample, diff_kv_sample)
```

```python
import functools
import math

import jax
import jax.numpy as jnp
from jax import lax
from jax.experimental import pallas as pl
from jax.experimental.pallas import tpu as pltpu

F32 = jnp.float32
BF16 = jnp.bfloat16

HEAD_DIM = 128
N_SB_HEADS = 8
N_DIFF_HEADS = 4
SB_WIDTH = N_SB_HEADS * HEAD_DIM
DIFF_WIDTH = N_DIFF_HEADS * 2 * HEAD_DIM
N_META = 16
ROT_DIM = HEAD_DIM // 4
ROPE_THETA = 500000.0
LN_EPS = 1e-5
DEPTH = 1
ALPHA = (2 * DEPTH) ** 0.25
LAM_INIT = 0.8 - 0.6 * math.exp(-0.3 * 0)
QK_SCALE = HEAD_DIM ** -0.5
LANES = 128
VMEM_LIMIT = 56 * 1024 * 1024

COL_SBQ, COL_SBK, COL_SBV = 0, SB_WIDTH, 2 * SB_WIDTH
COL_DQ, COL_DK, COL_DV = 3 * SB_WIDTH, 3 * SB_WIDTH + DIFF_WIDTH, 3 * SB_WIDTH + 2 * DIFF_WIDTH
QKV_COLS = 3 * SB_WIDTH + 3 * DIFF_WIDTH
SEG = 1024


def _nt_dot(a, b):
    return lax.dot_general(a, b, (((1,), (1,)), ((), ())), preferred_element_type=F32)


def _dot(a, b):
    return jnp.dot(a, b, preferred_element_type=F32)


def _rope_table(pos):
    half = ROT_DIM // 2
    inv = ROPE_THETA ** (-jnp.arange(half, dtype=F32) / half)
    ang = pos.astype(F32)[:, None] * inv[None, :]
    cos, sin = jnp.cos(ang), jnp.sin(ang)
    t = pos.shape[0]
    rest1 = jnp.ones((t, HEAD_DIM - ROT_DIM), F32)
    rest0 = jnp.zeros((t, HEAD_DIM - ROT_DIM), F32)
    z = jnp.zeros((t, half), F32)
    c = jnp.concatenate([cos, cos, rest1], axis=1)
    s1 = jnp.concatenate([-sin, z, rest0], axis=1)
    s2 = jnp.concatenate([z, sin, rest0], axis=1)
    return jnp.concatenate([c, s1, s2], axis=1)


def _rope(z, rope):
    c, s1, s2 = rope[:, :LANES], rope[:, LANES:2 * LANES], rope[:, 2 * LANES:]
    half = ROT_DIM // 2
    outs = []
    for i in range(z.shape[1] // LANES):
        zc = z[:, i * LANES:(i + 1) * LANES]
        outs.append(zc * c + pltpu.roll(zc, LANES - half, 1) * s1 + pltpu.roll(zc, half, 1) * s2)
    return jnp.concatenate(outs, axis=1)


def _proj_kernel(x_ref, w_ref, rope_ref, qkv_ref, sbkv_ref, dkv_ref, gate_ref, xb_ref, *, nb):
    j = pl.program_id(1)

    @pl.when(j == 0)
    def _():
        xb_ref[...] = x_ref[...].astype(BF16)

    z = _dot(xb_ref[...], w_ref[...])
    seg = j // nb

    @pl.when(seg == 0)
    def _():
        qkv_ref[...] = (z * QK_SCALE).astype(BF16)

    @pl.when((seg == 1) | (seg == 2))
    def _():
        qkv_ref[...] = z.astype(BF16)
        sbkv_ref[...] = z

    @pl.when(seg == 3)
    def _():
        qkv_ref[...] = (_rope(z, rope_ref[...]) * QK_SCALE).astype(BF16)

    @pl.when(seg == 4)
    def _():
        r = _rope(z, rope_ref[...])
        qkv_ref[...] = r.astype(BF16)
        dkv_ref[...] = r

    @pl.when(seg == 5)
    def _():
        qkv_ref[...] = z.astype(BF16)
        dkv_ref[...] = z

    @pl.when(seg >= 6)
    def _():
        gate_ref[...] = jax.nn.sigmoid(z).astype(BF16)


def _project(x, w_bf16, rope, *, bm, bn):
    m, d = x.shape
    n = w_bf16.shape[1]
    nb = SEG // bn
    grid = (m // bm, n // bn)
    return pl.pallas_call(
        functools.partial(_proj_kernel, nb=nb),
        grid=grid,
        in_specs=[
            pl.BlockSpec((bm, d), lambda i, j: (i, 0)),
            pl.BlockSpec((d, bn), lambda i, j: (0, j)),
            pl.BlockSpec((bm, 3 * LANES), lambda i, j: (i, 0)),
        ],
        out_specs=[
            pl.BlockSpec((bm, bn), lambda i, j: (i, jnp.minimum(j, 6 * nb - 1))),
            pl.BlockSpec((bm, bn), lambda i, j: (i, jnp.clip(j - nb, 0, 2 * nb - 1))),
            pl.BlockSpec((bm, bn), lambda i, j: (i, jnp.clip(j - 4 * nb, 0, 2 * nb - 1))),
            pl.BlockSpec((bm, bn), lambda i, j: (i, jnp.clip(j - 6 * nb, 0, 4 * nb - 1))),
        ],
        out_shape=[
            jax.ShapeDtypeStruct((m, QKV_COLS), BF16),
            jax.ShapeDtypeStruct((m, 2 * SB_WIDTH), F32),
            jax.ShapeDtypeStruct((m, 2 * DIFF_WIDTH), F32),
            jax.ShapeDtypeStruct((m, n - QKV_COLS), BF16),
        ],
        scratch_shapes=[pltpu.VMEM((bm, d), BF16)],
        compiler_params=pltpu.CompilerParams(
            dimension_semantics=("arbitrary", "arbitrary"), vmem_limit_bytes=VMEM_LIMIT),
        name="proj",
    )(x, w_bf16, rope)


def _split_dot(x, t):
    hi = x.astype(BF16)
    r1 = x - hi.astype(F32)
    mid = r1.astype(BF16)
    lo = (r1 - mid.astype(F32)).astype(BF16)
    return _dot(hi, t) + _dot(mid, t) + _dot(lo, t)


LOG_F32_ZERO = -104.0


def _sb_attn_kernel(q_ref, k_ref, v_ref, km_ref, vm_ref, tri_ref, o_ref, c_ref, acc_ref, *, bq, n_heads):
    qb = pl.program_id(1)
    tri = tri_ref[...]
    row = lax.broadcasted_iota(jnp.int32, (n_heads * bq, bq), 0) % bq
    col = lax.broadcasted_iota(jnp.int32, (n_heads * bq, bq), 1)
    head_cols = [slice(g * HEAD_DIM, (g + 1) * HEAD_DIM) for g in range(n_heads)]
    qs = [q_ref[:, hc] for hc in head_cols]

    def block(kv_rows, k_ref, v_ref, mask, c, acc):
        z = jnp.concatenate([_nt_dot(q, k_ref[kv_rows, hc]) for q, hc in zip(qs, head_cols)], axis=0)
        lk = -(jnp.maximum(z, 0.0) + jnp.log1p(jnp.exp(-jnp.abs(z))))
        if mask is not None:
            lk = jnp.where(mask, lk, 0.0)
        s2 = _split_dot(lk, tri)
        w = jnp.exp(z + lk + s2[:, :bq] + c)
        if mask is not None:
            w = jnp.where(mask, w, 0.0)
        w = w.astype(BF16)
        pv = [_dot(w[g * bq:(g + 1) * bq], v_ref[kv_rows, hc]) for g, hc in enumerate(head_cols)]
        return c + s2[:, bq:], acc + jnp.concatenate(pv, axis=0)

    start = pl.multiple_of(qb * bq, bq)
    c, acc = block(pl.ds(start, bq), k_ref, v_ref, col < row,
                   jnp.zeros((n_heads * bq, bq), F32), jnp.zeros((n_heads * bq, HEAD_DIM), F32))
    c_ref[...] = c
    acc_ref[...] = acc

    def live(carry):
        kb, c_max = carry
        return (kb >= 0) & (c_max > LOG_F32_ZERO)

    def body(carry):
        kb, _ = carry
        s = pl.multiple_of(kb * bq, bq)
        c, acc = block(pl.ds(s, bq), k_ref, v_ref, None, c_ref[...], acc_ref[...])
        c_ref[...] = c
        acc_ref[...] = acc
        return kb - 1, jnp.max(c)

    _, c_max = lax.while_loop(live, body, (qb - 1, jnp.max(c)))

    @pl.when(c_max > LOG_F32_ZERO)
    def _():
        _, acc = block(slice(None), km_ref, vm_ref, col < N_META, c_ref[...], acc_ref[...])
        acc_ref[...] = acc

    for g, hc in enumerate(head_cols):
        o_ref[:, hc] = acc_ref[g * bq:(g + 1) * bq, :].astype(o_ref.dtype)


def _sb_attention(qkv, qkv_meta, tri, *, bq, n_heads):
    t = qkv.shape[0]
    w = n_heads * HEAD_DIM
    kb, vb = COL_SBK // w, COL_SBV // w
    return pl.pallas_call(
        functools.partial(_sb_attn_kernel, bq=bq, n_heads=n_heads),
        grid=(N_SB_HEADS // n_heads, t // bq),
        in_specs=[
            pl.BlockSpec((bq, w), lambda h, i: (i, h)),
            pl.BlockSpec((t, w), lambda h, i: (0, kb + h)),
            pl.BlockSpec((t, w), lambda h, i: (0, vb + h)),
            pl.BlockSpec((bq, w), lambda h, i: (0, kb + h)),
            pl.BlockSpec((bq, w), lambda h, i: (0, vb + h)),
            pl.BlockSpec((bq, 2 * bq), lambda h, i: (0, 0)),
        ],
        out_specs=pl.BlockSpec((bq, w), lambda h, i: (i, h)),
        out_shape=jax.ShapeDtypeStruct((t, SB_WIDTH), BF16),
        scratch_shapes=[pltpu.VMEM((n_heads * bq, bq), F32), pltpu.VMEM((n_heads * bq, HEAD_DIM), F32)],
        compiler_params=pltpu.CompilerParams(
            dimension_semantics=("arbitrary", "arbitrary"), vmem_limit_bytes=VMEM_LIMIT),
        name="sb_attn",
    )(qkv, qkv, qkv, qkv_meta, qkv_meta, tri)


def _lambda(lq1_ref, lk1_ref, lq2_ref, lk2_ref):
    a = jnp.sum(lq1_ref[...] * lk1_ref[...], axis=-1, keepdims=True)
    b = jnp.sum(lq2_ref[...] * lk2_ref[...], axis=-1, keepdims=True)
    return jnp.exp(a) - jnp.exp(b) + LAM_INIT


def _head_norm(o, g):
    of = o * lax.rsqrt(jnp.mean(jnp.square(o), axis=-1, keepdims=True) + LN_EPS)
    return (of * g) * (1.0 - LAM_INIT)


def _diff_attn_kernel(lq1_ref, lk1_ref, lq2_ref, lk2_ref, g_ref, q1_ref, q2_ref, k1_ref, k2_ref, v_ref,
                      km1_ref, km2_ref, vm_ref, o_ref, *, bq, bk):
    qb = pl.program_id(1)
    lam = _lambda(lq1_ref, lk1_ref, lq2_ref, lk2_ref)
    q1, q2 = q1_ref[...], q2_ref[...]
    n_meta_rows = km1_ref.shape[0]
    n_full = qb * (bq // bk)
    meta_mask = lax.broadcasted_iota(jnp.int32, (bq, n_meta_rows), 1) < N_META
    row = lax.broadcasted_iota(jnp.int32, (bq, bk), 0)
    col = lax.broadcasted_iota(jnp.int32, (bq, bk), 1)

    def scores(k1, k2, mask):
        s1, s2 = _nt_dot(q1, k1), _nt_dot(q2, k2)
        if mask is not None:
            s1, s2 = jnp.where(mask, s1, -jnp.inf), jnp.where(mask, s2, -jnp.inf)
        return s1, s2

    def lanewise(op, x):
        out = x[:, :LANES]
        for j in range(1, x.shape[1] // LANES):
            out = op(out, x[:, j * LANES:(j + 1) * LANES])
        return out

    def key_block(start):
        s = pl.multiple_of(start, bk)
        return k1_ref[pl.ds(s, bk), :], k2_ref[pl.ds(s, bk), :], v_ref[pl.ds(s, bk), :]

    def diag_blocks():
        for j in range(bq // bk):
            yield key_block(qb * bq + j * bk) + (col + j * bk <= row,)

    def max_update(mx, k1, k2, mask):
        s1, s2 = scores(k1, k2, mask)
        return jnp.maximum(mx[0], lanewise(jnp.maximum, s1)), jnp.maximum(mx[1], lanewise(jnp.maximum, s2))

    neg = jnp.full((bq, LANES), -jnp.inf, F32)
    mx = max_update((neg, neg), km1_ref[...], km2_ref[...], meta_mask)
    mx = lax.fori_loop(0, n_full, lambda i, mx: max_update(mx, *key_block(i * bk)[:2], None), mx)
    for k1, k2, _, mask in diag_blocks():
        mx = max_update(mx, k1, k2, mask)
    m1 = jnp.max(mx[0], axis=-1, keepdims=True)
    m2 = jnp.max(mx[1], axis=-1, keepdims=True)

    def acc_update(st, k1, k2, v, mask):
        s1, s2 = scores(k1, k2, mask)
        p1, p2 = jnp.exp(s1 - m1), jnp.exp(s2 - m2)
        return (st[0] + lanewise(jnp.add, p1), st[1] + _dot(p1.astype(BF16), v),
                st[2] + lanewise(jnp.add, p2), st[3] + _dot(p2.astype(BF16), v))

    zl, za = jnp.zeros((bq, LANES), F32), jnp.zeros((bq, 2 * HEAD_DIM), F32)
    st = acc_update((zl, za, zl, za), km1_ref[...], km2_ref[...], vm_ref[...], meta_mask)
    st = lax.fori_loop(0, n_full, lambda i, st: acc_update(st, *key_block(i * bk), None), st)
    for k1, k2, v, mask in diag_blocks():
        st = acc_update(st, k1, k2, v, mask)
    l1 = jnp.sum(st[0], axis=-1, keepdims=True)
    l2 = jnp.sum(st[2], axis=-1, keepdims=True)
    o = st[1] / l1 - lam * (st[3] / l2)
    o_ref[...] = _head_norm(o, g_ref[...]).astype(o_ref.dtype)


def _diff_attention(qkv, qkv_meta, lams, diff_norm_g, *, bq, bk):
    t = qkv.shape[0]
    mrows = qkv_meta.shape[0]
    qb, kb, vb = COL_DQ // HEAD_DIM, COL_DK // HEAD_DIM, COL_DV // (2 * HEAD_DIM)
    vec = pl.BlockSpec((1, HEAD_DIM), lambda h, i: (0, 0))
    return pl.pallas_call(
        functools.partial(_diff_attn_kernel, bq=bq, bk=bk),
        grid=(N_DIFF_HEADS, t // bq),
        in_specs=[
            vec, vec, vec, vec,
            pl.BlockSpec((1, 2 * HEAD_DIM), lambda h, i: (0, 0)),
            pl.BlockSpec((bq, HEAD_DIM), lambda h, i: (i, qb + 2 * h)),
            pl.BlockSpec((bq, HEAD_DIM), lambda h, i: (i, qb + 2 * h + 1)),
            pl.BlockSpec((t, HEAD_DIM), lambda h, i: (0, kb + 2 * h)),
            pl.BlockSpec((t, HEAD_DIM), lambda h, i: (0, kb + 2 * h + 1)),
            pl.BlockSpec((t, 2 * HEAD_DIM), lambda h, i: (0, vb + h)),
            pl.BlockSpec((mrows, HEAD_DIM), lambda h, i: (0, kb + 2 * h)),
            pl.BlockSpec((mrows, HEAD_DIM), lambda h, i: (0, kb + 2 * h + 1)),
            pl.BlockSpec((mrows, 2 * HEAD_DIM), lambda h, i: (0, vb + h)),
        ],
        out_specs=pl.BlockSpec((bq, 2 * HEAD_DIM), lambda h, i: (i, h)),
        out_shape=jax.ShapeDtypeStruct((t, DIFF_WIDTH), BF16),
        compiler_params=pltpu.CompilerParams(
            dimension_semantics=("arbitrary", "arbitrary"), vmem_limit_bytes=VMEM_LIMIT),
        name="diff_attn",
    )(*lams, diff_norm_g, qkv, qkv, qkv, qkv, qkv, qkv_meta, qkv_meta, qkv_meta)


def _layer_norm(x, g, b):
    mu = jnp.mean(x, axis=-1, keepdims=True)
    xc = x - mu
    var = jnp.mean(jnp.square(xc), axis=-1, keepdims=True)
    return (xc * lax.rsqrt(var + LN_EPS)) * g + b


def _merge_kernel(x_ref, osb_ref, od_ref, gsb_ref, gd_ref, wsb_ref, wd_ref, wo_ref, g_ref, b_ref, o_ref):
    br_sb = _dot(osb_ref[...], wsb_ref[...])
    br_d = _dot(od_ref[...], wd_ref[...])
    gated = gsb_ref[...].astype(F32) * br_sb + gd_ref[...].astype(F32) * br_d
    mix = _dot(gated.astype(BF16), wo_ref[...])
    o_ref[...] = _layer_norm(ALPHA * x_ref[...] + mix, g_ref[...], b_ref[...])


def _merge(x, o_sb, o_d, gates, w_sb, w_d, w_o, ln_g, ln_b, *, bm):
    m, d = x.shape
    row = lambda i: (i, 0)
    const = lambda i: (0, 0)
    return pl.pallas_call(
        _merge_kernel,
        grid=(m // bm,),
        in_specs=[
            pl.BlockSpec((bm, d), row),
            pl.BlockSpec((bm, SB_WIDTH), row),
            pl.BlockSpec((bm, DIFF_WIDTH), row),
            pl.BlockSpec((bm, d), row),
            pl.BlockSpec((bm, d), lambda i: (i, 1)),
            pl.BlockSpec((SB_WIDTH, d), const),
            pl.BlockSpec((DIFF_WIDTH, d), const),
            pl.BlockSpec((d, d), const),
            pl.BlockSpec((1, d), const),
            pl.BlockSpec((1, d), const),
        ],
        out_specs=pl.BlockSpec((bm, d), row),
        out_shape=jax.ShapeDtypeStruct((m, d), F32),
        compiler_params=pltpu.CompilerParams(
            dimension_semantics=("arbitrary",), vmem_limit_bytes=VMEM_LIMIT),
        name="merge",
    )(x, o_sb, o_d, gates, gates, w_sb, w_d, w_o, ln_g, ln_b)


def _ffn_kernel(x_ref, wu_ref, wd_ref, g_ref, b_ref, o_ref, xb_ref, acc_ref):
    f = pl.program_id(1)

    @pl.when(f == 0)
    def _():
        xb_ref[...] = x_ref[...].astype(BF16)
        acc_ref[...] = jnp.zeros_like(acc_ref)

    u = _dot(xb_ref[...], wu_ref[...])
    a = jnp.square(jnp.maximum(u, 0.0)).astype(BF16)
    acc_ref[...] += _dot(a, wd_ref[...])

    @pl.when(f == pl.num_programs(1) - 1)
    def _():
        o_ref[...] = _layer_norm(ALPHA * x_ref[...] + acc_ref[...], g_ref[...], b_ref[...])


def _ffn(x, w_up, w_down, ln_g, ln_b, *, bm, bf):
    m, d = x.shape
    dff = w_up.shape[1]
    return pl.pallas_call(
        _ffn_kernel,
        grid=(m // bm, dff // bf),
        in_specs=[
            pl.BlockSpec((bm, d), lambda i, f: (i, 0)),
            pl.BlockSpec((d, bf), lambda i, f: (0, f)),
            pl.BlockSpec((bf, d), lambda i, f: (f, 0)),
            pl.BlockSpec((1, d), lambda i, f: (0, 0)),
            pl.BlockSpec((1, d), lambda i, f: (0, 0)),
        ],
        out_specs=pl.BlockSpec((bm, d), lambda i, f: (i, 0)),
        out_shape=jax.ShapeDtypeStruct((m, d), F32),
        scratch_shapes=[pltpu.VMEM((bm, d), BF16), pltpu.VMEM((bm, d), F32)],
        compiler_params=pltpu.CompilerParams(
            dimension_semantics=("arbitrary", "arbitrary"), vmem_limit_bytes=VMEM_LIMIT),
        name="ffn",
    )(x, w_up, w_down, ln_g, ln_b)


N_ROWS = 8


def _sample_attn_kernel(pt_ref, lq1_ref, lk1_ref, lq2_ref, lk2_ref, g_ref, qsb_ref, qd_ref, kself_ref, vself_ref,
                        lt_ref, *refs, n_seq):
    sb_hbm, dc_refs = refs[0], refs[1:1 + n_seq]
    osb_ref, od_ref, c_ref, asb_ref, m_ref, l_ref, ad_ref, sb_buf, sb_sem, issued_ref = refs[1 + n_seq:]
    i, p = pl.program_id(0), pl.program_id(1)
    n_grp, n_pages = pl.num_programs(0), pl.num_programs(1)
    page = dc_refs[0].shape[0]
    n = page * N_ROWS
    n_col = n // LANES
    n_row = n_seq * N_ROWS
    half = N_ROWS // 2
    rows = lax.broadcasted_iota(jnp.int32, (n_row, n), 0) % N_ROWS
    lanes = lax.broadcasted_iota(jnp.int32, (n_row, n), 1)
    own = (lanes % N_ROWS) == rows
    slot = p % 2

    def page_matrix(ref, kv):
        return ref[:, kv].reshape(n, HEAD_DIM).astype(BF16)

    def sb_copy(grp, pg, slot_, u):
        start = pl.multiple_of(pt_ref[grp * n_seq + u, n_pages - 1 - pg] * page, page)
        return pltpu.make_async_copy(sb_hbm.at[pl.ds(start, page)], sb_buf.at[slot_, u], sb_sem.at[slot_, u])

    @pl.when(p == 0)
    def _():
        c_ref[...] = jnp.zeros_like(c_ref)
        asb_ref[...] = jnp.zeros_like(asb_ref)
        l_ref[...] = jnp.ones_like(l_ref)
        ad_ref[...] = vself_ref[...]
        qk = qd_ref[...].astype(F32) * kself_ref[...].astype(F32)
        s_self = jnp.sum(qk.reshape(n_row, HEAD_DIM), axis=-1, keepdims=True)
        m_ref[...] = jnp.broadcast_to(s_self, m_ref.shape)

    @pl.when((i == 0) & (p == 0))
    def _():
        for u in range(n_seq):
            sb_copy(0, 0, 0, u).start()
        issued_ref[0] = 1

    alive = jnp.max(c_ref[...]) > LOG_F32_ZERO
    last_page = p == n_pages - 1
    fetch_next = jnp.where(last_page, i + 1 < n_grp, alive)

    @pl.when(fetch_next)
    def _():
        grp = jnp.where(last_page, i + 1, i)
        pg = jnp.where(last_page, 0, p + 1)
        for u in range(n_seq):
            sb_copy(grp, pg, 1 - slot, u).start()

    issued_ref[1 - slot] = fetch_next.astype(jnp.int32)

    @pl.when(issued_ref[slot] == 1)
    def _():
        for u in range(n_seq):
            sb_copy(i, p, slot, u).wait()

    @pl.when(alive)
    def _():
        sbc_refs = [sb_buf.at[slot, u] for u in range(n_seq)]
        g = jnp.concatenate([_nt_dot(qsb_ref[u], page_matrix(sbc_refs[u], 0)) for u in range(n_seq)], axis=0)
        lk = jnp.where(own, -(jnp.maximum(g, 0.0) + jnp.log1p(jnp.exp(-jnp.abs(g)))), 0.0)
        lk_cols = jnp.concatenate([lk[:, j * LANES:(j + 1) * LANES] for j in range(n_col)], axis=0)
        s2 = _split_dot(lk_cols, lt_ref[...])
        c = c_ref[...]
        ws = [None] * n_col
        for j in reversed(range(n_col)):
            sl = slice(j * LANES, (j + 1) * LANES)
            e = g[:, sl] + lk[:, sl] + s2[j * n_row:(j + 1) * n_row, :LANES] + c
            ws[j] = jnp.where(own[:, sl], jnp.exp(e), 0.0)
            c = c + s2[j * n_row:(j + 1) * n_row, LANES:]
        c_ref[...] = jnp.where(own[:, :LANES], c, -jnp.inf)
        w = jnp.concatenate(ws, axis=1).astype(BF16)
        for u in range(n_seq):
            asb_ref[u] += _dot(w[u * N_ROWS:(u + 1) * N_ROWS], page_matrix(sbc_refs[u], 1))

    s = jnp.concatenate([_nt_dot(qd_ref[u], page_matrix(dc_refs[u], 0)) for u in range(n_seq)], axis=0)
    s = jnp.where(own, s, -jnp.inf)
    m = m_ref[...][:, :1]
    m_new = jnp.maximum(m, jnp.max(s, axis=-1, keepdims=True))
    a = jnp.exp(m - m_new)
    pr = jnp.exp(s - m_new)
    l_ref[...] = a * l_ref[...] + jnp.sum(pr, axis=-1, keepdims=True)
    m_ref[...] = jnp.broadcast_to(m_new, m_ref.shape)
    s_other = jnp.where(rows < half, pltpu.roll(s, half, 1), pltpu.roll(s, n - half, 1))
    pr_other = jnp.exp(s_other - m_new)
    for u in range(n_seq):
        sl = slice(u * N_ROWS, (u + 1) * N_ROWS)
        lhs = jnp.concatenate([pr[sl], pr_other[sl]], axis=0).astype(BF16)
        ad_ref[u] = jnp.concatenate([a[sl], a[sl]], axis=0) * ad_ref[u] + _dot(lhs, page_matrix(dc_refs[u], 1))

    @pl.when(p == pl.num_programs(1) - 1)
    def _():
        osb_ref[...] = asb_ref[...].astype(osb_ref.dtype)
        lam = _lambda(lq1_ref, lk1_ref, lq2_ref, lk2_ref)
        gain = g_ref[...]
        for u in range(n_seq):
            l = l_ref[u * N_ROWS:(u + 1) * N_ROWS, :1]
            x = ad_ref[u] / jnp.concatenate([l, l], axis=0)
            lo = x[0:half] - lam * x[3 * half:4 * half]
            hi = x[2 * half:3 * half] - lam * x[half:2 * half]
            ms = (jnp.sum(jnp.square(lo), axis=-1, keepdims=True)
                  + jnp.sum(jnp.square(hi), axis=-1, keepdims=True)) / (2 * HEAD_DIM)
            r = lax.rsqrt(ms + LN_EPS)
            od_ref[u] = jnp.concatenate([((lo * r) * gain[:, :HEAD_DIM]) * (1.0 - LAM_INIT),
                                         ((hi * r) * gain[:, HEAD_DIM:]) * (1.0 - LAM_INIT)],
                                        axis=1).astype(od_ref.dtype)


def _column_suffix_matrix():
    i = jnp.arange(LANES)
    same = (i[:, None] % N_ROWS) == (i[None, :] % N_ROWS)
    later = (i[:, None] // N_ROWS) > (i[None, :] // N_ROWS)
    return jnp.concatenate([same & later, same], axis=1).astype(BF16)


def _sample_attention(page_table, qkv_s, cache_sb, cache_d, lams, diff_norm_g, *, page, n_seq):
    b, n_pages = page_table.shape
    half = N_DIFF_HEADS

    def by_half(x):
        return x.reshape(b, N_DIFF_HEADS, 2, HEAD_DIM).transpose(0, 2, 1, 3).reshape(b, N_ROWS, HEAD_DIM)

    q_sb = qkv_s[:, COL_SBQ:COL_SBQ + SEG].reshape(b, N_ROWS, HEAD_DIM)
    q_d = by_half(qkv_s[:, COL_DQ:COL_DQ + SEG])
    k_self = by_half(qkv_s[:, COL_DK:COL_DK + SEG])
    v = by_half(qkv_s[:, COL_DV:COL_DV + SEG]).astype(F32)
    v_lo, v_hi = v[:, :half], v[:, half:]
    v_self = jnp.concatenate([v_lo, v_hi, v_hi, v_lo], axis=1)

    vec = pl.BlockSpec((1, HEAD_DIM), lambda i, p, pt: (0, 0))
    per_seq = lambda i, p, pt: (i, 0, 0)

    def page_spec(u):
        return pl.BlockSpec((page, 2, N_ROWS, HEAD_DIM),
                            lambda i, p, pt: (pt[i * n_seq + u, n_pages - 1 - p], 0, 0, 0))

    grid_spec = pltpu.PrefetchScalarGridSpec(
        num_scalar_prefetch=1,
        grid=(b // n_seq, n_pages),
        in_specs=[
            vec, vec, vec, vec,
            pl.BlockSpec((1, 2 * HEAD_DIM), lambda i, p, pt: (0, 0)),
            pl.BlockSpec((n_seq, N_ROWS, HEAD_DIM), per_seq),
            pl.BlockSpec((n_seq, N_ROWS, HEAD_DIM), per_seq),
            pl.BlockSpec((n_seq, N_ROWS, HEAD_DIM), per_seq),
            pl.BlockSpec((n_seq, 2 * N_ROWS, HEAD_DIM), per_seq),
            pl.BlockSpec((LANES, 2 * LANES), lambda i, p, pt: (0, 0)),
            pl.BlockSpec(memory_space=pl.ANY),
        ] + [page_spec(u) for u in range(n_seq)],
        out_specs=[
            pl.BlockSpec((n_seq, N_SB_HEADS, HEAD_DIM), per_seq),
            pl.BlockSpec((n_seq, N_DIFF_HEADS, 2 * HEAD_DIM), per_seq),
        ],
        scratch_shapes=[
            pltpu.VMEM((n_seq * N_ROWS, LANES), F32),
            pltpu.VMEM((n_seq, N_ROWS, HEAD_DIM), F32),
            pltpu.VMEM((n_seq * N_ROWS, LANES), F32),
            pltpu.VMEM((n_seq * N_ROWS, LANES), F32),
            pltpu.VMEM((n_seq, 2 * N_ROWS, HEAD_DIM), F32),
            pltpu.VMEM((2, n_seq, page, 2, N_ROWS, HEAD_DIM), F32),
            pltpu.SemaphoreType.DMA((2, n_seq)),
            pltpu.SMEM((2,), jnp.int32),
        ],
    )
    o_sb, o_d = pl.pallas_call(
        functools.partial(_sample_attn_kernel, n_seq=n_seq),
        grid_spec=grid_spec,
        out_shape=[
            jax.ShapeDtypeStruct((b, N_SB_HEADS, HEAD_DIM), BF16),
            jax.ShapeDtypeStruct((b, N_DIFF_HEADS, 2 * HEAD_DIM), BF16),
        ],
        compiler_params=pltpu.CompilerParams(
            dimension_semantics=("arbitrary", "arbitrary"), vmem_limit_bytes=VMEM_LIMIT),
        name="sample_attn",
    )(page_table, *lams, diff_norm_g, q_sb, q_d, k_self, v_self, _column_suffix_matrix(),
      cache_sb, *([cache_d] * n_seq))
    return o_sb.reshape(b, SB_WIDTH), o_d.reshape(b, DIFF_WIDTH)


def _suffix_matrix(n):
    j = jnp.arange(n)[:, None]
    s = jnp.arange(n)[None, :]
    return jnp.concatenate([(j > s), jnp.ones((n, n), bool)], axis=1).astype(BF16)


def _pick(m, candidates):
    for c in candidates:
        if m % c == 0:
            return c
    raise ValueError(f"no block size for {m}")


def kernel(x_prompt, x_sample, cache_sb_kv, cache_diff_kv, page_table, meta_tokens, w_in, lambda_q1, lambda_k1,
           lambda_q2, lambda_k2, diff_norm_g, w_branch_sb, w_branch_diff, w_out, ln_mix_g, ln_mix_b, w_up,
           w_down, ln_ffn_g, ln_ffn_b):
    assert w_in.shape[0] == DEPTH and x_prompt.shape[0] == 1 and x_sample.shape[1] == 1
    seq, d = x_prompt.shape[1:]
    n_dec = x_sample.shape[0]
    n_pool, page = cache_sb_kv.shape[1:3]
    past_len = page_table.shape[1] * page
    bq = 128

    w_in_b = w_in[0].astype(BF16)
    w_sb_b, w_d_b, w_o_b = w_branch_sb[0].astype(BF16), w_branch_diff[0].astype(BF16), w_out[0].astype(BF16)
    w_up_b, w_down_b = w_up[0].astype(BF16), w_down[0].astype(BF16)
    lams = (lambda_q1, lambda_k1, lambda_q2, lambda_k2)

    xp = x_prompt[0]
    rope_p = _rope_table(N_META + jnp.arange(seq))
    qkv_p, sbkv_p, dkv_p, gates_p = _project(xp, w_in_b, rope_p, bm=_pick(seq, (1024, 512, 256, 128)), bn=512)
    x_small = jnp.concatenate([meta_tokens.astype(F32), x_sample[:, 0]], axis=0)
    pos_small = jnp.concatenate([jnp.arange(N_META), jnp.full((n_dec,), past_len)])
    qkv_s, sbkv_s, dkv_s, gates_s = _project(x_small, w_in_b, _rope_table(pos_small), bm=N_META + n_dec, bn=512)

    tri = _suffix_matrix(bq)
    qkv_meta = jnp.concatenate([qkv_s[:N_META], jnp.zeros((bq - N_META, QKV_COLS), BF16)], axis=0)
    o_sb = _sb_attention(qkv_p, qkv_meta, tri, bq=bq, n_heads=4)
    bqd = _pick(seq, (512, 256, 128))
    o_d = _diff_attention(qkv_p, qkv_meta, lams, diff_norm_g, bq=bqd, bk=bqd)
    x1 = _merge(xp, o_sb, o_d, gates_p, w_sb_b, w_d_b, w_o_b, ln_mix_g, ln_mix_b, bm=_pick(seq, (256, 128)))
    y_prompt = _ffn(x1, w_up_b, w_down_b, ln_ffn_g, ln_ffn_b, bm=_pick(seq, (512, 256, 128)), bf=512)

    cache_sb = cache_sb_kv.reshape(n_pool * page, 2, N_SB_HEADS, HEAD_DIM)
    cache_d = cache_diff_kv.reshape(n_pool * page, 2, N_DIFF_HEADS, 2, HEAD_DIM).transpose(0, 1, 3, 2, 4).reshape(
        n_pool * page, 2, N_ROWS, HEAD_DIM)
    os_sb, os_d = _sample_attention(page_table, qkv_s[N_META:], cache_sb, cache_d, lams, diff_norm_g, page=page,
                                    n_seq=_pick(n_dec, (4, 2, 1)))
    xs1 = _merge(x_sample[:, 0], os_sb, os_d, gates_s[N_META:], w_sb_b, w_d_b, w_o_b, ln_mix_g, ln_mix_b,
                 bm=n_dec)
    y_sample = _ffn(xs1, w_up_b, w_down_b, ln_ffn_g, ln_ffn_b, bm=n_dec, bf=512)

    t_p = N_META + seq
    sb_kv_prompt = jnp.concatenate([sbkv_s[:N_META], sbkv_p], axis=0).reshape(1, 1, t_p, 2, N_SB_HEADS, HEAD_DIM)
    diff_kv_prompt = jnp.concatenate([dkv_s[:N_META], dkv_p], axis=0).reshape(
        1, 1, t_p, 2, N_DIFF_HEADS, 2 * HEAD_DIM)
    sb_kv_sample = sbkv_s[N_META:].reshape(1, n_dec, 1, 2, N_SB_HEADS, HEAD_DIM)
    diff_kv_sample = dkv_s[N_META:].reshape(1, n_dec, 1, 2, N_DIFF_HEADS, 2 * HEAD_DIM)
    return (y_prompt[None], y_sample[:, None], sb_kv_prompt, diff_kv_prompt, sb_kv_sample, diff_kv_sample)
```

```python
import functools
import math

import jax
import jax.numpy as jnp
from jax import lax
from jax.experimental import pallas as pl
from jax.experimental.pallas import tpu as pltpu

F32 = jnp.float32
BF16 = jnp.bfloat16

HEAD_DIM = 128
N_SB_HEADS = 8
N_DIFF_HEADS = 4
SB_WIDTH = N_SB_HEADS * HEAD_DIM
DIFF_WIDTH = N_DIFF_HEADS * 2 * HEAD_DIM
N_META = 16
ROT_DIM = HEAD_DIM // 4
ROPE_THETA = 500000.0
LN_EPS = 1e-5
DEPTH = 1
ALPHA = (2 * DEPTH) ** 0.25
LAM_INIT = 0.8 - 0.6 * math.exp(-0.3 * 0)
QK_SCALE = HEAD_DIM ** -0.5
LANES = 128
VMEM_LIMIT = 56 * 1024 * 1024

COL_SBQ, COL_SBK, COL_SBV = 0, SB_WIDTH, 2 * SB_WIDTH
COL_DQ, COL_DK, COL_DV = 3 * SB_WIDTH, 3 * SB_WIDTH + DIFF_WIDTH, 3 * SB_WIDTH + 2 * DIFF_WIDTH
QKV_COLS = 3 * SB_WIDTH + 3 * DIFF_WIDTH
SEG = 1024


def _nt_dot(a, b):
    return lax.dot_general(a, b, (((1,), (1,)), ((), ())), preferred_element_type=F32)


def _dot(a, b):
    return jnp.dot(a, b, preferred_element_type=F32)


def _rope_table(pos):
    half = ROT_DIM // 2
    inv = ROPE_THETA ** (-jnp.arange(half, dtype=F32) / half)
    ang = pos.astype(F32)[:, None] * inv[None, :]
    cos, sin = jnp.cos(ang), jnp.sin(ang)
    t = pos.shape[0]
    rest1 = jnp.ones((t, HEAD_DIM - ROT_DIM), F32)
    rest0 = jnp.zeros((t, HEAD_DIM - ROT_DIM), F32)
    z = jnp.zeros((t, half), F32)
    c = jnp.concatenate([cos, cos, rest1], axis=1)
    s1 = jnp.concatenate([-sin, z, rest0], axis=1)
    s2 = jnp.concatenate([z, sin, rest0], axis=1)
    return jnp.concatenate([c, s1, s2], axis=1)


def _rope(z, rope):
    c, s1, s2 = rope[:, :LANES], rope[:, LANES:2 * LANES], rope[:, 2 * LANES:]
    half = ROT_DIM // 2
    outs = []
    for i in range(z.shape[1] // LANES):
        zc = z[:, i * LANES:(i + 1) * LANES]
        outs.append(zc * c + pltpu.roll(zc, LANES - half, 1) * s1 + pltpu.roll(zc, half, 1) * s2)
    return jnp.concatenate(outs, axis=1)


def _proj_kernel(x_ref, w_ref, rope_ref, qkv_ref, sbkv_ref, dkv_ref, gate_ref, xb_ref, *, nb):
    j = pl.program_id(1)

    @pl.when(j == 0)
    def _():
        xb_ref[...] = x_ref[...].astype(BF16)

    z = _dot(xb_ref[...], w_ref[...])
    seg = j // nb

    @pl.when(seg == 0)
    def _():
        qkv_ref[...] = (z * QK_SCALE).astype(BF16)

    @pl.when((seg == 1) | (seg == 2))
    def _():
        qkv_ref[...] = z.astype(BF16)
        sbkv_ref[...] = z

    @pl.when(seg == 3)
    def _():
        qkv_ref[...] = (_rope(z, rope_ref[...]) * QK_SCALE).astype(BF16)

    @pl.when(seg == 4)
    def _():
        r = _rope(z, rope_ref[...])
        qkv_ref[...] = r.astype(BF16)
        dkv_ref[...] = r

    @pl.when(seg == 5)
    def _():
        qkv_ref[...] = z.astype(BF16)
        dkv_ref[...] = z

    @pl.when(seg >= 6)
    def _():
        gate_ref[...] = jax.nn.sigmoid(z).astype(BF16)


def _project(x, w_bf16, rope, *, bm, bn):
    m, d = x.shape
    n = w_bf16.shape[1]
    nb = SEG // bn
    grid = (m // bm, n // bn)
    return pl.pallas_call(
        functools.partial(_proj_kernel, nb=nb),
        grid=grid,
        in_specs=[
            pl.BlockSpec((bm, d), lambda i, j: (i, 0)),
            pl.BlockSpec((d, bn), lambda i, j: (0, j)),
            pl.BlockSpec((bm, 3 * LANES), lambda i, j: (i, 0)),
        ],
        out_specs=[
            pl.BlockSpec((bm, bn), lambda i, j: (i, jnp.minimum(j, 6 * nb - 1))),
            pl.BlockSpec((bm, bn), lambda i, j: (i, jnp.clip(j - nb, 0, 2 * nb - 1))),
            pl.BlockSpec((bm, bn), lambda i, j: (i, jnp.clip(j - 4 * nb, 0, 2 * nb - 1))),
            pl.BlockSpec((bm, bn), lambda i, j: (i, jnp.clip(j - 6 * nb, 0, 4 * nb - 1))),
        ],
        out_shape=[
            jax.ShapeDtypeStruct((m, QKV_COLS), BF16),
            jax.ShapeDtypeStruct((m, 2 * SB_WIDTH), F32),
            jax.ShapeDtypeStruct((m, 2 * DIFF_WIDTH), F32),
            jax.ShapeDtypeStruct((m, n - QKV_COLS), BF16),
        ],
        scratch_shapes=[pltpu.VMEM((bm, d), BF16)],
        compiler_params=pltpu.CompilerParams(
            dimension_semantics=("arbitrary", "arbitrary"), vmem_limit_bytes=VMEM_LIMIT),
        name="proj",
    )(x, w_bf16, rope)


def _split_dot(x, t):
    hi = x.astype(BF16)
    r1 = x - hi.astype(F32)
    mid = r1.astype(BF16)
    lo = (r1 - mid.astype(F32)).astype(BF16)
    return _dot(hi, t) + _dot(mid, t) + _dot(lo, t)


LOG_F32_ZERO = -104.0


def _sb_attn_kernel(q_ref, k_ref, v_ref, km_ref, vm_ref, tri_ref, o_ref, c_ref, acc_ref, *, bq, n_heads):
    qb = pl.program_id(1)
    tri = tri_ref[...]
    row = lax.broadcasted_iota(jnp.int32, (n_heads * bq, bq), 0) % bq
    col = lax.broadcasted_iota(jnp.int32, (n_heads * bq, bq), 1)
    head_cols = [slice(g * HEAD_DIM, (g + 1) * HEAD_DIM) for g in range(n_heads)]
    qs = [q_ref[:, hc] for hc in head_cols]

    def block(kv_rows, k_ref, v_ref, mask, c, acc):
        z = jnp.concatenate([_nt_dot(q, k_ref[kv_rows, hc]) for q, hc in zip(qs, head_cols)], axis=0)
        lk = -(jnp.maximum(z, 0.0) + jnp.log1p(jnp.exp(-jnp.abs(z))))
        if mask is not None:
            lk = jnp.where(mask, lk, 0.0)
        s2 = _split_dot(lk, tri)
        w = jnp.exp(z + lk + s2[:, :bq] + c)
        if mask is not None:
            w = jnp.where(mask, w, 0.0)
        w = w.astype(BF16)
        pv = [_dot(w[g * bq:(g + 1) * bq], v_ref[kv_rows, hc]) for g, hc in enumerate(head_cols)]
        return c + s2[:, bq:], acc + jnp.concatenate(pv, axis=0)

    start = pl.multiple_of(qb * bq, bq)
    c, acc = block(pl.ds(start, bq), k_ref, v_ref, col < row,
                   jnp.zeros((n_heads * bq, bq), F32), jnp.zeros((n_heads * bq, HEAD_DIM), F32))
    c_ref[...] = c
    acc_ref[...] = acc

    def live(carry):
        kb, c_max = carry
        return (kb >= 0) & (c_max > LOG_F32_ZERO)

    def body(carry):
        kb, _ = carry
        s = pl.multiple_of(kb * bq, bq)
        c, acc = block(pl.ds(s, bq), k_ref, v_ref, None, c_ref[...], acc_ref[...])
        c_ref[...] = c
        acc_ref[...] = acc
        return kb - 1, jnp.max(c)

    _, c_max = lax.while_loop(live, body, (qb - 1, jnp.max(c)))

    @pl.when(c_max > LOG_F32_ZERO)
    def _():
        _, acc = block(slice(None), km_ref, vm_ref, col < N_META, c_ref[...], acc_ref[...])
        acc_ref[...] = acc

    for g, hc in enumerate(head_cols):
        o_ref[:, hc] = acc_ref[g * bq:(g + 1) * bq, :].astype(o_ref.dtype)


def _sb_attention(qkv, qkv_meta, tri, *, bq, n_heads):
    t = qkv.shape[0]
    w = n_heads * HEAD_DIM
    kb, vb = COL_SBK // w, COL_SBV // w
    return pl.pallas_call(
        functools.partial(_sb_attn_kernel, bq=bq, n_heads=n_heads),
        grid=(N_SB_HEADS // n_heads, t // bq),
        in_specs=[
            pl.BlockSpec((bq, w), lambda h, i: (i, h)),
            pl.BlockSpec((t, w), lambda h, i: (0, kb + h)),
            pl.BlockSpec((t, w), lambda h, i: (0, vb + h)),
            pl.BlockSpec((bq, w), lambda h, i: (0, kb + h)),
            pl.BlockSpec((bq, w), lambda h, i: (0, vb + h)),
            pl.BlockSpec((bq, 2 * bq), lambda h, i: (0, 0)),
        ],
        out_specs=pl.BlockSpec((bq, w), lambda h, i: (i, h)),
        out_shape=jax.ShapeDtypeStruct((t, SB_WIDTH), BF16),
        scratch_shapes=[pltpu.VMEM((n_heads * bq, bq), F32), pltpu.VMEM((n_heads * bq, HEAD_DIM), F32)],
        compiler_params=pltpu.CompilerParams(
            dimension_semantics=("arbitrary", "arbitrary"), vmem_limit_bytes=VMEM_LIMIT),
        name="sb_attn",
    )(qkv, qkv, qkv, qkv_meta, qkv_meta, tri)


def _lambda(lq1_ref, lk1_ref, lq2_ref, lk2_ref):
    a = jnp.sum(lq1_ref[...] * lk1_ref[...], axis=-1, keepdims=True)
    b = jnp.sum(lq2_ref[...] * lk2_ref[...], axis=-1, keepdims=True)
    return jnp.exp(a) - jnp.exp(b) + LAM_INIT


def _head_norm(o, g):
    of = o * lax.rsqrt(jnp.mean(jnp.square(o), axis=-1, keepdims=True) + LN_EPS)
    return (of * g) * (1.0 - LAM_INIT)


def _diff_attn_kernel(lq1_ref, lk1_ref, lq2_ref, lk2_ref, g_ref, q1_ref, q2_ref, k1_ref, k2_ref, v_ref,
                      km1_ref, km2_ref, vm_ref, o_ref, *, bq, bk):
    qb = pl.program_id(1)
    lam = _lambda(lq1_ref, lk1_ref, lq2_ref, lk2_ref)
    q1, q2 = q1_ref[...], q2_ref[...]
    n_meta_rows = km1_ref.shape[0]
    n_full = qb * (bq // bk)
    meta_mask = lax.broadcasted_iota(jnp.int32, (bq, n_meta_rows), 1) < N_META
    row = lax.broadcasted_iota(jnp.int32, (bq, bk), 0)
    col = lax.broadcasted_iota(jnp.int32, (bq, bk), 1)

    def scores(k1, k2, mask):
        s1, s2 = _nt_dot(q1, k1), _nt_dot(q2, k2)
        if mask is not None:
            s1, s2 = jnp.where(mask, s1, -jnp.inf), jnp.where(mask, s2, -jnp.inf)
        return s1, s2

    def lanewise(op, x):
        out = x[:, :LANES]
        for j in range(1, x.shape[1] // LANES):
            out = op(out, x[:, j * LANES:(j + 1) * LANES])
        return out

    def key_block(start):
        s = pl.multiple_of(start, bk)
        return k1_ref[pl.ds(s, bk), :], k2_ref[pl.ds(s, bk), :], v_ref[pl.ds(s, bk), :]

    def diag_blocks():
        for j in range(bq // bk):
            yield key_block(qb * bq + j * bk) + (col + j * bk <= row,)

    def max_update(mx, k1, k2, mask):
        s1, s2 = scores(k1, k2, mask)
        return jnp.maximum(mx[0], lanewise(jnp.maximum, s1)), jnp.maximum(mx[1], lanewise(jnp.maximum, s2))

    neg = jnp.full((bq, LANES), -jnp.inf, F32)
    mx = max_update((neg, neg), km1_ref[...], km2_ref[...], meta_mask)
    mx = lax.fori_loop(0, n_full, lambda i, mx: max_update(mx, *key_block(i * bk)[:2], None), mx)
    for k1, k2, _, mask in diag_blocks():
        mx = max_update(mx, k1, k2, mask)
    m1 = jnp.max(mx[0], axis=-1, keepdims=True)
    m2 = jnp.max(mx[1], axis=-1, keepdims=True)

    def acc_update(st, k1, k2, v, mask):
        s1, s2 = scores(k1, k2, mask)
        p1, p2 = jnp.exp(s1 - m1), jnp.exp(s2 - m2)
        return (st[0] + lanewise(jnp.add, p1), st[1] + _dot(p1.astype(BF16), v),
                st[2] + lanewise(jnp.add, p2), st[3] + _dot(p2.astype(BF16), v))

    zl, za = jnp.zeros((bq, LANES), F32), jnp.zeros((bq, 2 * HEAD_DIM), F32)
    st = acc_update((zl, za, zl, za), km1_ref[...], km2_ref[...], vm_ref[...], meta_mask)
    st = lax.fori_loop(0, n_full, lambda i, st: acc_update(st, *key_block(i * bk), None), st)
    for k1, k2, v, mask in diag_blocks():
        st = acc_update(st, k1, k2, v, mask)
    l1 = jnp.sum(st[0], axis=-1, keepdims=True)
    l2 = jnp.sum(st[2], axis=-1, keepdims=True)
    o = st[1] / l1 - lam * (st[3] / l2)
    o_ref[...] = _head_norm(o, g_ref[...]).astype(o_ref.dtype)


def _diff_attention(qkv, qkv_meta, lams, diff_norm_g, *, bq, bk):
    t = qkv.shape[0]
    mrows = qkv_meta.shape[0]
    qb, kb, vb = COL_DQ // HEAD_DIM, COL_DK // HEAD_DIM, COL_DV // (2 * HEAD_DIM)
    vec = pl.BlockSpec((1, HEAD_DIM), lambda h, i: (0, 0))
    return pl.pallas_call(
        functools.partial(_diff_attn_kernel, bq=bq, bk=bk),
        grid=(N_DIFF_HEADS, t // bq),
        in_specs=[
            vec, vec, vec, vec,
            pl.BlockSpec((1, 2 * HEAD_DIM), lambda h, i: (0, 0)),
            pl.BlockSpec((bq, HEAD_DIM), lambda h, i: (i, qb + 2 * h)),
            pl.BlockSpec((bq, HEAD_DIM), lambda h, i: (i, qb + 2 * h + 1)),
            pl.BlockSpec((t, HEAD_DIM), lambda h, i: (0, kb + 2 * h)),
            pl.BlockSpec((t, HEAD_DIM), lambda h, i: (0, kb + 2 * h + 1)),
            pl.BlockSpec((t, 2 * HEAD_DIM), lambda h, i: (0, vb + h)),
            pl.BlockSpec((mrows, HEAD_DIM), lambda h, i: (0, kb + 2 * h)),
            pl.BlockSpec((mrows, HEAD_DIM), lambda h, i: (0, kb + 2 * h + 1)),
            pl.BlockSpec((mrows, 2 * HEAD_DIM), lambda h, i: (0, vb + h)),
        ],
        out_specs=pl.BlockSpec((bq, 2 * HEAD_DIM), lambda h, i: (i, h)),
        out_shape=jax.ShapeDtypeStruct((t, DIFF_WIDTH), BF16),
        compiler_params=pltpu.CompilerParams(
            dimension_semantics=("arbitrary", "arbitrary"), vmem_limit_bytes=VMEM_LIMIT),
        name="diff_attn",
    )(*lams, diff_norm_g, qkv, qkv, qkv, qkv, qkv, qkv_meta, qkv_meta, qkv_meta)


def _layer_norm(x, g, b):
    mu = jnp.mean(x, axis=-1, keepdims=True)
    xc = x - mu
    var = jnp.mean(jnp.square(xc), axis=-1, keepdims=True)
    return (xc * lax.rsqrt(var + LN_EPS)) * g + b


def _merge_kernel(x_ref, osb_ref, od_ref, gsb_ref, gd_ref, wsb_ref, wd_ref, wo_ref, g_ref, b_ref, o_ref):
    br_sb = _dot(osb_ref[...], wsb_ref[...])
    br_d = _dot(od_ref[...], wd_ref[...])
    gated = gsb_ref[...].astype(F32) * br_sb + gd_ref[...].astype(F32) * br_d
    mix = _dot(gated.astype(BF16), wo_ref[...])
    o_ref[...] = _layer_norm(ALPHA * x_ref[...] + mix, g_ref[...], b_ref[...])


def _merge(x, o_sb, o_d, gates, w_sb, w_d, w_o, ln_g, ln_b, *, bm):
    m, d = x.shape
    row = lambda i: (i, 0)
    const = lambda i: (0, 0)
    return pl.pallas_call(
        _merge_kernel,
        grid=(m // bm,),
        in_specs=[
            pl.BlockSpec((bm, d), row),
            pl.BlockSpec((bm, SB_WIDTH), row),
            pl.BlockSpec((bm, DIFF_WIDTH), row),
            pl.BlockSpec((bm, d), row),
            pl.BlockSpec((bm, d), lambda i: (i, 1)),
            pl.BlockSpec((SB_WIDTH, d), const),
            pl.BlockSpec((DIFF_WIDTH, d), const),
            pl.BlockSpec((d, d), const),
            pl.BlockSpec((1, d), const),
            pl.BlockSpec((1, d), const),
        ],
        out_specs=pl.BlockSpec((bm, d), row),
        out_shape=jax.ShapeDtypeStruct((m, d), F32),
        compiler_params=pltpu.CompilerParams(
            dimension_semantics=("arbitrary",), vmem_limit_bytes=VMEM_LIMIT),
        name="merge",
    )(x, o_sb, o_d, gates, gates, w_sb, w_d, w_o, ln_g, ln_b)


def _ffn_kernel(x_ref, wu_ref, wd_ref, g_ref, b_ref, o_ref, xb_ref, acc_ref):
    f = pl.program_id(1)

    @pl.when(f == 0)
    def _():
        xb_ref[...] = x_ref[...].astype(BF16)
        acc_ref[...] = jnp.zeros_like(acc_ref)

    u = _dot(xb_ref[...], wu_ref[...])
    a = jnp.square(jnp.maximum(u, 0.0)).astype(BF16)
    acc_ref[...] += _dot(a, wd_ref[...])

    @pl.when(f == pl.num_programs(1) - 1)
    def _():
        o_ref[...] = _layer_norm(ALPHA * x_ref[...] + acc_ref[...], g_ref[...], b_ref[...])


def _ffn(x, w_up, w_down, ln_g, ln_b, *, bm, bf):
    m, d = x.shape
    dff = w_up.shape[1]
    return pl.pallas_call(
        _ffn_kernel,
        grid=(m // bm, dff // bf),
        in_specs=[
            pl.BlockSpec((bm, d), lambda i, f: (i, 0)),
            pl.BlockSpec((d, bf), lambda i, f: (0, f)),
            pl.BlockSpec((bf, d), lambda i, f: (f, 0)),
            pl.BlockSpec((1, d), lambda i, f: (0, 0)),
            pl.BlockSpec((1, d), lambda i, f: (0, 0)),
        ],
        out_specs=pl.BlockSpec((bm, d), lambda i, f: (i, 0)),
        out_shape=jax.ShapeDtypeStruct((m, d), F32),
        scratch_shapes=[pltpu.VMEM((bm, d), BF16), pltpu.VMEM((bm, d), F32)],
        compiler_params=pltpu.CompilerParams(
            dimension_semantics=("arbitrary", "arbitrary"), vmem_limit_bytes=VMEM_LIMIT),
        name="ffn",
    )(x, w_up, w_down, ln_g, ln_b)


N_ROWS = 8


def _sample_attn_kernel(pt_ref, lq1_ref, lk1_ref, lq2_ref, lk2_ref, g_ref, qsb_ref, qd_ref, kself_ref, vself_ref,
                        lt_ref, *refs, n_seq):
    sb_hbm, dc_refs = refs[0], refs[1:1 + n_seq]
    osb_ref, od_ref, c_ref, asb_ref, m_ref, l_ref, ad_ref, sb_buf, sb_sem, issued_ref = refs[1 + n_seq:]
    i, p = pl.program_id(0), pl.program_id(1)
    n_grp, n_pages = pl.num_programs(0), pl.num_programs(1)
    page = dc_refs[0].shape[0]
    n = page * N_ROWS
    n_col = n // LANES
    n_row = n_seq * N_ROWS
    half = N_ROWS // 2
    rows = lax.broadcasted_iota(jnp.int32, (n_row, n), 0) % N_ROWS
    lanes = lax.broadcasted_iota(jnp.int32, (n_row, n), 1)
    own = (lanes % N_ROWS) == rows
    slot = p % 2

    def page_matrix(ref, kv):
        return ref[:, kv].reshape(n, HEAD_DIM).astype(BF16)

    def sb_copy(grp, pg, slot_, u):
        start = pl.multiple_of(pt_ref[grp * n_seq + u, n_pages - 1 - pg] * page, page)
        return pltpu.make_async_copy(sb_hbm.at[pl.ds(start, page)], sb_buf.at[slot_, u], sb_sem.at[slot_, u])

    @pl.when(p == 0)
    def _():
        c_ref[...] = jnp.zeros_like(c_ref)
        asb_ref[...] = jnp.zeros_like(asb_ref)
        l_ref[...] = jnp.ones_like(l_ref)
        ad_ref[...] = vself_ref[...]
        qk = qd_ref[...].astype(F32) * kself_ref[...].astype(F32)
        s_self = jnp.sum(qk.reshape(n_row, HEAD_DIM), axis=-1, keepdims=True)
        m_ref[...] = jnp.broadcast_to(s_self, m_ref.shape)

    @pl.when((i == 0) & (p == 0))
    def _():
        for u in range(n_seq):
            sb_copy(0, 0, 0, u).start()
        issued_ref[0] = 1

    alive = jnp.max(c_ref[...]) > LOG_F32_ZERO
    last_page = p == n_pages - 1
    fetch_next = jnp.where(last_page, i + 1 < n_grp, alive)

    @pl.when(fetch_next)
    def _():
        grp = jnp.where(last_page, i + 1, i)
        pg = jnp.where(last_page, 0, p + 1)
        for u in range(n_seq):
            sb_copy(grp, pg, 1 - slot, u).start()

    issued_ref[1 - slot] = fetch_next.astype(jnp.int32)

    @pl.when(issued_ref[slot] == 1)
    def _():
        for u in range(n_seq):
            sb_copy(i, p, slot, u).wait()

    @pl.when(alive)
    def _():
        sbc_refs = [sb_buf.at[slot, u] for u in range(n_seq)]
        g = jnp.concatenate([_nt_dot(qsb_ref[u], page_matrix(sbc_refs[u], 0)) for u in range(n_seq)], axis=0)
        lk = jnp.where(own, -(jnp.maximum(g, 0.0) + jnp.log1p(jnp.exp(-jnp.abs(g)))), 0.0)
        lk_cols = jnp.concatenate([lk[:, j * LANES:(j + 1) * LANES] for j in range(n_col)], axis=0)
        s2 = _split_dot(lk_cols, lt_ref[...])
        c = c_ref[...]
        ws = [None] * n_col
        for j in reversed(range(n_col)):
            sl = slice(j * LANES, (j + 1) * LANES)
            e = g[:, sl] + lk[:, sl] + s2[j * n_row:(j + 1) * n_row, :LANES] + c
            ws[j] = jnp.where(own[:, sl], jnp.exp(e), 0.0)
            c = c + s2[j * n_row:(j + 1) * n_row, LANES:]
        c_ref[...] = jnp.where(own[:, :LANES], c, -jnp.inf)
        w = jnp.concatenate(ws, axis=1).astype(BF16)
        for u in range(n_seq):
            asb_ref[u] += _dot(w[u * N_ROWS:(u + 1) * N_ROWS], page_matrix(sbc_refs[u], 1))

    s = jnp.concatenate([_nt_dot(qd_ref[u], page_matrix(dc_refs[u], 0)) for u in range(n_seq)], axis=0)
    s = jnp.where(own, s, -jnp.inf)
    m = m_ref[...][:, :1]
    m_new = jnp.maximum(m, jnp.max(s, axis=-1, keepdims=True))
    a = jnp.exp(m - m_new)
    pr = jnp.exp(s - m_new)
    l_ref[...] = a * l_ref[...] + jnp.sum(pr, axis=-1, keepdims=True)
    m_ref[...] = jnp.broadcast_to(m_new, m_ref.shape)
    s_other = jnp.where(rows < half, pltpu.roll(s, half, 1), pltpu.roll(s, n - half, 1))
    pr_other = jnp.exp(s_other - m_new)
    for u in range(n_seq):
        sl = slice(u * N_ROWS, (u + 1) * N_ROWS)
        lhs = jnp.concatenate([pr[sl], pr_other[sl]], axis=0).astype(BF16)
        ad_ref[u] = jnp.concatenate([a[sl], a[sl]], axis=0) * ad_ref[u] + _dot(lhs, page_matrix(dc_refs[u], 1))

    @pl.when(p == pl.num_programs(1) - 1)
    def _():
        osb_ref[...] = asb_ref[...].astype(osb_ref.dtype)
        lam = _lambda(lq1_ref, lk1_ref, lq2_ref, lk2_ref)
        gain = g_ref[...]
        for u in range(n_seq):
            l = l_ref[u * N_ROWS:(u + 1) * N_ROWS, :1]
            x = ad_ref[u] / jnp.concatenate([l, l], axis=0)
            lo = x[0:half] - lam * x[3 * half:4 * half]
            hi = x[2 * half:3 * half] - lam * x[half:2 * half]
            ms = (jnp.sum(jnp.square(lo), axis=-1, keepdims=True)
                  + jnp.sum(jnp.square(hi), axis=-1, keepdims=True)) / (2 * HEAD_DIM)
            r = lax.rsqrt(ms + LN_EPS)
            od_ref[u] = jnp.concatenate([((lo * r) * gain[:, :HEAD_DIM]) * (1.0 - LAM_INIT),
                                         ((hi * r) * gain[:, HEAD_DIM:]) * (1.0 - LAM_INIT)],
                                        axis=1).astype(od_ref.dtype)


def _column_suffix_matrix():
    i = jnp.arange(LANES)
    same = (i[:, None] % N_ROWS) == (i[None, :] % N_ROWS)
    later = (i[:, None] // N_ROWS) > (i[None, :] // N_ROWS)
    return jnp.concatenate([same & later, same], axis=1).astype(BF16)


def _sample_attention(page_table, qkv_s, cache_sb, cache_d, lams, diff_norm_g, *, page, n_seq):
    b, n_pages = page_table.shape
    half = N_DIFF_HEADS

    def by_half(x):
        return x.reshape(b, N_DIFF_HEADS, 2, HEAD_DIM).transpose(0, 2, 1, 3).reshape(b, N_ROWS, HEAD_DIM)

    q_sb = qkv_s[:, COL_SBQ:COL_SBQ + SEG].reshape(b, N_ROWS, HEAD_DIM)
    q_d = by_half(qkv_s[:, COL_DQ:COL_DQ + SEG])
    k_self = by_half(qkv_s[:, COL_DK:COL_DK + SEG])
    v = by_half(qkv_s[:, COL_DV:COL_DV + SEG]).astype(F32)
    v_lo, v_hi = v[:, :half], v[:, half:]
    v_self = jnp.concatenate([v_lo, v_hi, v_hi, v_lo], axis=1)

    vec = pl.BlockSpec((1, HEAD_DIM), lambda i, p, pt: (0, 0))
    per_seq = lambda i, p, pt: (i, 0, 0)

    def page_spec(u):
        return pl.BlockSpec((page, 2, N_ROWS, HEAD_DIM),
                            lambda i, p, pt: (pt[i * n_seq + u, n_pages - 1 - p], 0, 0, 0))

    grid_spec = pltpu.PrefetchScalarGridSpec(
        num_scalar_prefetch=1,
        grid=(b // n_seq, n_pages),
        in_specs=[
            vec, vec, vec, vec,
            pl.BlockSpec((1, 2 * HEAD_DIM), lambda i, p, pt: (0, 0)),
            pl.BlockSpec((n_seq, N_ROWS, HEAD_DIM), per_seq),
            pl.BlockSpec((n_seq, N_ROWS, HEAD_DIM), per_seq),
            pl.BlockSpec((n_seq, N_ROWS, HEAD_DIM), per_seq),
            pl.BlockSpec((n_seq, 2 * N_ROWS, HEAD_DIM), per_seq),
            pl.BlockSpec((LANES, 2 * LANES), lambda i, p, pt: (0, 0)),
            pl.BlockSpec(memory_space=pl.ANY),
        ] + [page_spec(u) for u in range(n_seq)],
        out_specs=[
            pl.BlockSpec((n_seq, N_SB_HEADS, HEAD_DIM), per_seq),
            pl.BlockSpec((n_seq, N_DIFF_HEADS, 2 * HEAD_DIM), per_seq),
        ],
        scratch_shapes=[
            pltpu.VMEM((n_seq * N_ROWS, LANES), F32),
            pltpu.VMEM((n_seq, N_ROWS, HEAD_DIM), F32),
            pltpu.VMEM((n_seq * N_ROWS, LANES), F32),
            pltpu.VMEM((n_seq * N_ROWS, LANES), F32),
            pltpu.VMEM((n_seq, 2 * N_ROWS, HEAD_DIM), F32),
            pltpu.VMEM((2, n_seq, page, 2, N_ROWS, HEAD_DIM), F32),
            pltpu.SemaphoreType.DMA((2, n_seq)),
            pltpu.SMEM((2,), jnp.int32),
        ],
    )
    o_sb, o_d = pl.pallas_call(
        functools.partial(_sample_attn_kernel, n_seq=n_seq),
        grid_spec=grid_spec,
        out_shape=[
            jax.ShapeDtypeStruct((b, N_SB_HEADS, HEAD_DIM), BF16),
            jax.ShapeDtypeStruct((b, N_DIFF_HEADS, 2 * HEAD_DIM), BF16),
        ],
        compiler_params=pltpu.CompilerParams(
            dimension_semantics=("arbitrary", "arbitrary"), vmem_limit_bytes=VMEM_LIMIT),
        name="sample_attn",
    )(page_table, *lams, diff_norm_g, q_sb, q_d, k_self, v_self, _column_suffix_matrix(),
      cache_sb, *([cache_d] * n_seq))
    return o_sb.reshape(b, SB_WIDTH), o_d.reshape(b, DIFF_WIDTH)


def _suffix_matrix(n):
    j = jnp.arange(n)[:, None]
    s = jnp.arange(n)[None, :]
    return jnp.concatenate([(j > s), jnp.ones((n, n), bool)], axis=1).astype(BF16)


def _pick(m, candidates):
    for c in candidates:
        if m % c == 0:
            return c
    raise ValueError(f"no block size for {m}")


def kernel(x_prompt, x_sample, cache_sb_kv, cache_diff_kv, page_table, meta_tokens, w_in, lambda_q1, lambda_k1,
           lambda_q2, lambda_k2, diff_norm_g, w_branch_sb, w_branch_diff, w_out, ln_mix_g, ln_mix_b, w_up,
           w_down, ln_ffn_g, ln_ffn_b):
    assert w_in.shape[0] == DEPTH and x_prompt.shape[0] == 1 and x_sample.shape[1] == 1
    seq, d = x_prompt.shape[1:]
    n_dec = x_sample.shape[0]
    n_pool, page = cache_sb_kv.shape[1:3]
    past_len = page_table.shape[1] * page
    bq = 128

    w_in_b = w_in[0].astype(BF16)
    w_sb_b, w_d_b, w_o_b = w_branch_sb[0].astype(BF16), w_branch_diff[0].astype(BF16), w_out[0].astype(BF16)
    w_up_b, w_down_b = w_up[0].astype(BF16), w_down[0].astype(BF16)
    lams = (lambda_q1, lambda_k1, lambda_q2, lambda_k2)

    xp = x_prompt[0]
    rope_p = _rope_table(N_META + jnp.arange(seq))
    qkv_p, sbkv_p, dkv_p, gates_p = _project(xp, w_in_b, rope_p, bm=_pick(seq, (1024, 512, 256, 128)), bn=512)
    x_small = jnp.concatenate([meta_tokens.astype(F32), x_sample[:, 0]], axis=0)
    pos_small = jnp.concatenate([jnp.arange(N_META), jnp.full((n_dec,), past_len)])
    qkv_s, sbkv_s, dkv_s, gates_s = _project(x_small, w_in_b, _rope_table(pos_small), bm=N_META + n_dec, bn=512)

    tri = _suffix_matrix(bq)
    qkv_meta = jnp.concatenate([qkv_s[:N_META], jnp.zeros((bq - N_META, QKV_COLS), BF16)], axis=0)
    o_sb = _sb_attention(qkv_p, qkv_meta, tri, bq=bq, n_heads=4)
    bqd = _pick(seq, (1024, 512, 256, 128))
    o_d = _diff_attention(qkv_p, qkv_meta, lams, diff_norm_g, bq=bqd, bk=bqd)
    x1 = _merge(xp, o_sb, o_d, gates_p, w_sb_b, w_d_b, w_o_b, ln_mix_g, ln_mix_b, bm=_pick(seq, (256, 128)))
    y_prompt = _ffn(x1, w_up_b, w_down_b, ln_ffn_g, ln_ffn_b, bm=_pick(seq, (512, 256, 128)), bf=512)

    cache_sb = cache_sb_kv.reshape(n_pool * page, 2, N_SB_HEADS, HEAD_DIM)
    cache_d = cache_diff_kv.reshape(n_pool * page, 2, N_DIFF_HEADS, 2, HEAD_DIM).transpose(0, 1, 3, 2, 4).reshape(
        n_pool * page, 2, N_ROWS, HEAD_DIM)
    os_sb, os_d = _sample_attention(page_table, qkv_s[N_META:], cache_sb, cache_d, lams, diff_norm_g, page=page,
                                    n_seq=_pick(n_dec, (8, 4, 2, 1)))
    xs1 = _merge(x_sample[:, 0], os_sb, os_d, gates_s[N_META:], w_sb_b, w_d_b, w_o_b, ln_mix_g, ln_mix_b,
                 bm=n_dec)
    y_sample = _ffn(xs1, w_up_b, w_down_b, ln_ffn_g, ln_ffn_b, bm=n_dec, bf=512)

    t_p = N_META + seq
    sb_kv_prompt = jnp.concatenate([sbkv_s[:N_META], sbkv_p], axis=0).reshape(1, 1, t_p, 2, N_SB_HEADS, HEAD_DIM)
    diff_kv_prompt = jnp.concatenate([dkv_s[:N_META], dkv_p], axis=0).reshape(
        1, 1, t_p, 2, N_DIFF_HEADS, 2 * HEAD_DIM)
    sb_kv_sample = sbkv_s[N_META:].reshape(1, n_dec, 1, 2, N_SB_HEADS, HEAD_DIM)
    diff_kv_sample = dkv_s[N_META:].reshape(1, n_dec, 1, 2, N_DIFF_HEADS, 2 * HEAD_DIM)
    return (y_prompt[None], y_sample[:, None], sb_kv_prompt, diff_kv_prompt, sb_kv_sample, diff_kv_sample)
```

```python
import functools
import math

import jax
import jax.numpy as jnp
from jax import lax
from jax.experimental import pallas as pl
from jax.experimental.pallas import tpu as pltpu

F32 = jnp.float32
BF16 = jnp.bfloat16

HEAD_DIM = 128
N_SB_HEADS = 8
N_DIFF_HEADS = 4
SB_WIDTH = N_SB_HEADS * HEAD_DIM
DIFF_WIDTH = N_DIFF_HEADS * 2 * HEAD_DIM
N_META = 16
ROT_DIM = HEAD_DIM // 4
ROPE_THETA = 500000.0
LN_EPS = 1e-5
DEPTH = 1
ALPHA = (2 * DEPTH) ** 0.25
LAM_INIT = 0.8 - 0.6 * math.exp(-0.3 * 0)
QK_SCALE = HEAD_DIM ** -0.5
LANES = 128
VMEM_LIMIT = 56 * 1024 * 1024

COL_SBQ, COL_SBK, COL_SBV = 0, SB_WIDTH, 2 * SB_WIDTH
COL_DQ, COL_DK, COL_DV = 3 * SB_WIDTH, 3 * SB_WIDTH + DIFF_WIDTH, 3 * SB_WIDTH + 2 * DIFF_WIDTH
QKV_COLS = 3 * SB_WIDTH + 3 * DIFF_WIDTH
SEG = 1024


def _nt_dot(a, b):
    return lax.dot_general(a, b, (((1,), (1,)), ((), ())), preferred_element_type=F32)


def _dot(a, b):
    return jnp.dot(a, b, preferred_element_type=F32)


def _rope_table(pos):
    half = ROT_DIM // 2
    inv = ROPE_THETA ** (-jnp.arange(half, dtype=F32) / half)
    ang = pos.astype(F32)[:, None] * inv[None, :]
    cos, sin = jnp.cos(ang), jnp.sin(ang)
    t = pos.shape[0]
    rest1 = jnp.ones((t, HEAD_DIM - ROT_DIM), F32)
    rest0 = jnp.zeros((t, HEAD_DIM - ROT_DIM), F32)
    z = jnp.zeros((t, half), F32)
    c = jnp.concatenate([cos, cos, rest1], axis=1)
    s1 = jnp.concatenate([-sin, z, rest0], axis=1)
    s2 = jnp.concatenate([z, sin, rest0], axis=1)
    return jnp.concatenate([c, s1, s2], axis=1)


def _rope(z, rope):
    c, s1, s2 = rope[:, :LANES], rope[:, LANES:2 * LANES], rope[:, 2 * LANES:]
    half = ROT_DIM // 2
    outs = []
    for i in range(z.shape[1] // LANES):
        zc = z[:, i * LANES:(i + 1) * LANES]
        outs.append(zc * c + pltpu.roll(zc, LANES - half, 1) * s1 + pltpu.roll(zc, half, 1) * s2)
    return jnp.concatenate(outs, axis=1)


def _proj_kernel(x_ref, w_ref, rope_ref, qkv_ref, sbkv_ref, dkv_ref, gate_ref, xb_ref, *, nb):
    j = pl.program_id(1)

    @pl.when(j == 0)
    def _():
        xb_ref[...] = x_ref[...].astype(BF16)

    z = _dot(xb_ref[...], w_ref[...])
    seg = j // nb

    @pl.when(seg == 0)
    def _():
        qkv_ref[...] = (z * QK_SCALE).astype(BF16)

    @pl.when((seg == 1) | (seg == 2))
    def _():
        qkv_ref[...] = z.astype(BF16)
        sbkv_ref[...] = z

    @pl.when(seg == 3)
    def _():
        qkv_ref[...] = (_rope(z, rope_ref[...]) * QK_SCALE).astype(BF16)

    @pl.when(seg == 4)
    def _():
        r = _rope(z, rope_ref[...])
        qkv_ref[...] = r.astype(BF16)
        dkv_ref[...] = r

    @pl.when(seg == 5)
    def _():
        qkv_ref[...] = z.astype(BF16)
        dkv_ref[...] = z

    @pl.when(seg >= 6)
    def _():
        gate_ref[...] = jax.nn.sigmoid(z).astype(BF16)


def _project(x, w_bf16, rope, *, bm, bn):
    m, d = x.shape
    n = w_bf16.shape[1]
    nb = SEG // bn
    grid = (m // bm, n // bn)
    return pl.pallas_call(
        functools.partial(_proj_kernel, nb=nb),
        grid=grid,
        in_specs=[
            pl.BlockSpec((bm, d), lambda i, j: (i, 0)),
            pl.BlockSpec((d, bn), lambda i, j: (0, j)),
            pl.BlockSpec((bm, 3 * LANES), lambda i, j: (i, 0)),
        ],
        out_specs=[
            pl.BlockSpec((bm, bn), lambda i, j: (i, jnp.minimum(j, 6 * nb - 1))),
            pl.BlockSpec((bm, bn), lambda i, j: (i, jnp.clip(j - nb, 0, 2 * nb - 1))),
            pl.BlockSpec((bm, bn), lambda i, j: (i, jnp.clip(j - 4 * nb, 0, 2 * nb - 1))),
            pl.BlockSpec((bm, bn), lambda i, j: (i, jnp.clip(j - 6 * nb, 0, 4 * nb - 1))),
        ],
        out_shape=[
            jax.ShapeDtypeStruct((m, QKV_COLS), BF16),
            jax.ShapeDtypeStruct((m, 2 * SB_WIDTH), F32),
            jax.ShapeDtypeStruct((m, 2 * DIFF_WIDTH), F32),
            jax.ShapeDtypeStruct((m, n - QKV_COLS), BF16),
        ],
        scratch_shapes=[pltpu.VMEM((bm, d), BF16)],
        compiler_params=pltpu.CompilerParams(
            dimension_semantics=("arbitrary", "arbitrary"), vmem_limit_bytes=VMEM_LIMIT),
        name="proj",
    )(x, w_bf16, rope)


def _split_dot(x, t):
    hi = x.astype(BF16)
    r1 = x - hi.astype(F32)
    mid = r1.astype(BF16)
    lo = (r1 - mid.astype(F32)).astype(BF16)
    return _dot(hi, t) + _dot(mid, t) + _dot(lo, t)


LOG_F32_ZERO = -104.0


def _sb_attn_kernel(q_ref, k_ref, v_ref, km_ref, vm_ref, tri_ref, o_ref, c_ref, acc_ref, *, bq, n_heads):
    qb = pl.program_id(1)
    tri = tri_ref[...]
    row = lax.broadcasted_iota(jnp.int32, (n_heads * bq, bq), 0) % bq
    col = lax.broadcasted_iota(jnp.int32, (n_heads * bq, bq), 1)
    head_cols = [slice(g * HEAD_DIM, (g + 1) * HEAD_DIM) for g in range(n_heads)]
    qs = [q_ref[:, hc] for hc in head_cols]

    def block(kv_rows, k_ref, v_ref, mask, c, acc):
        z = jnp.concatenate([_nt_dot(q, k_ref[kv_rows, hc]) for q, hc in zip(qs, head_cols)], axis=0)
        lk = -(jnp.maximum(z, 0.0) + jnp.log1p(jnp.exp(-jnp.abs(z))))
        if mask is not None:
            lk = jnp.where(mask, lk, 0.0)
        s2 = _split_dot(lk, tri)
        w = jnp.exp(z + lk + s2[:, :bq] + c)
        if mask is not None:
            w = jnp.where(mask, w, 0.0)
        w = w.astype(BF16)
        pv = [_dot(w[g * bq:(g + 1) * bq], v_ref[kv_rows, hc]) for g, hc in enumerate(head_cols)]
        return c + s2[:, bq:], acc + jnp.concatenate(pv, axis=0)

    start = pl.multiple_of(qb * bq, bq)
    c, acc = block(pl.ds(start, bq), k_ref, v_ref, col < row,
                   jnp.zeros((n_heads * bq, bq), F32), jnp.zeros((n_heads * bq, HEAD_DIM), F32))
    c_ref[...] = c
    acc_ref[...] = acc

    def live(carry):
        kb, c_max = carry
        return (kb >= 0) & (c_max > LOG_F32_ZERO)

    def body(carry):
        kb, _ = carry
        s = pl.multiple_of(kb * bq, bq)
        c, acc = block(pl.ds(s, bq), k_ref, v_ref, None, c_ref[...], acc_ref[...])
        c_ref[...] = c
        acc_ref[...] = acc
        return kb - 1, jnp.max(c)

    _, c_max = lax.while_loop(live, body, (qb - 1, jnp.max(c)))

    @pl.when(c_max > LOG_F32_ZERO)
    def _():
        _, acc = block(slice(None), km_ref, vm_ref, col < N_META, c_ref[...], acc_ref[...])
        acc_ref[...] = acc

    for g, hc in enumerate(head_cols):
        o_ref[:, hc] = acc_ref[g * bq:(g + 1) * bq, :].astype(o_ref.dtype)


def _sb_attention(qkv, qkv_meta, tri, *, bq, n_heads):
    t = qkv.shape[0]
    w = n_heads * HEAD_DIM
    kb, vb = COL_SBK // w, COL_SBV // w
    return pl.pallas_call(
        functools.partial(_sb_attn_kernel, bq=bq, n_heads=n_heads),
        grid=(N_SB_HEADS // n_heads, t // bq),
        in_specs=[
            pl.BlockSpec((bq, w), lambda h, i: (i, h)),
            pl.BlockSpec((t, w), lambda h, i: (0, kb + h)),
            pl.BlockSpec((t, w), lambda h, i: (0, vb + h)),
            pl.BlockSpec((bq, w), lambda h, i: (0, kb + h)),
            pl.BlockSpec((bq, w), lambda h, i: (0, vb + h)),
            pl.BlockSpec((bq, 2 * bq), lambda h, i: (0, 0)),
        ],
        out_specs=pl.BlockSpec((bq, w), lambda h, i: (i, h)),
        out_shape=jax.ShapeDtypeStruct((t, SB_WIDTH), BF16),
        scratch_shapes=[pltpu.VMEM((n_heads * bq, bq), F32), pltpu.VMEM((n_heads * bq, HEAD_DIM), F32)],
        compiler_params=pltpu.CompilerParams(
            dimension_semantics=("arbitrary", "arbitrary"), vmem_limit_bytes=VMEM_LIMIT),
        name="sb_attn",
    )(qkv, qkv, qkv, qkv_meta, qkv_meta, tri)


def _lambda(lq1_ref, lk1_ref, lq2_ref, lk2_ref):
    a = jnp.sum(lq1_ref[...] * lk1_ref[...], axis=-1, keepdims=True)
    b = jnp.sum(lq2_ref[...] * lk2_ref[...], axis=-1, keepdims=True)
    return jnp.exp(a) - jnp.exp(b) + LAM_INIT


def _head_norm(o, g):
    of = o * lax.rsqrt(jnp.mean(jnp.square(o), axis=-1, keepdims=True) + LN_EPS)
    return (of * g) * (1.0 - LAM_INIT)


def _diff_attn_kernel(lq1_ref, lk1_ref, lq2_ref, lk2_ref, g_ref, q1_ref, q2_ref, k1_ref, k2_ref, v_ref,
                      km1_ref, km2_ref, vm_ref, o_ref, *, bq, bk):
    qb = pl.program_id(1)
    lam = _lambda(lq1_ref, lk1_ref, lq2_ref, lk2_ref)
    q1, q2 = q1_ref[...], q2_ref[...]
    n_meta_rows = km1_ref.shape[0]
    n_full = qb * (bq // bk)
    meta_mask = lax.broadcasted_iota(jnp.int32, (bq, n_meta_rows), 1) < N_META
    row = lax.broadcasted_iota(jnp.int32, (bq, bk), 0)
    col = lax.broadcasted_iota(jnp.int32, (bq, bk), 1)

    def scores(k1, k2, mask):
        s1, s2 = _nt_dot(q1, k1), _nt_dot(q2, k2)
        if mask is not None:
            s1, s2 = jnp.where(mask, s1, -jnp.inf), jnp.where(mask, s2, -jnp.inf)
        return s1, s2

    def lanewise(op, x):
        out = x[:, :LANES]
        for j in range(1, x.shape[1] // LANES):
            out = op(out, x[:, j * LANES:(j + 1) * LANES])
        return out

    def key_block(start):
        s = pl.multiple_of(start, bk)
        return k1_ref[pl.ds(s, bk), :], k2_ref[pl.ds(s, bk), :], v_ref[pl.ds(s, bk), :]

    def diag_blocks():
        for j in range(bq // bk):
            yield key_block(qb * bq + j * bk) + (col + j * bk <= row,)

    def max_update(mx, k1, k2, mask):
        s1, s2 = scores(k1, k2, mask)
        return jnp.maximum(mx[0], lanewise(jnp.maximum, s1)), jnp.maximum(mx[1], lanewise(jnp.maximum, s2))

    neg = jnp.full((bq, LANES), -jnp.inf, F32)
    mx = max_update((neg, neg), km1_ref[...], km2_ref[...], meta_mask)
    mx = lax.fori_loop(0, n_full, lambda i, mx: max_update(mx, *key_block(i * bk)[:2], None), mx)
    for k1, k2, _, mask in diag_blocks():
        mx = max_update(mx, k1, k2, mask)
    m1 = jnp.max(mx[0], axis=-1, keepdims=True)
    m2 = jnp.max(mx[1], axis=-1, keepdims=True)

    def acc_update(st, k1, k2, v, mask):
        s1, s2 = scores(k1, k2, mask)
        p1, p2 = jnp.exp(s1 - m1), jnp.exp(s2 - m2)
        return (st[0] + lanewise(jnp.add, p1), st[1] + _dot(p1.astype(BF16), v),
                st[2] + lanewise(jnp.add, p2), st[3] + _dot(p2.astype(BF16), v))

    zl, za = jnp.zeros((bq, LANES), F32), jnp.zeros((bq, 2 * HEAD_DIM), F32)
    st = acc_update((zl, za, zl, za), km1_ref[...], km2_ref[...], vm_ref[...], meta_mask)
    st = lax.fori_loop(0, n_full, lambda i, st: acc_update(st, *key_block(i * bk), None), st)
    for k1, k2, v, mask in diag_blocks():
        st = acc_update(st, k1, k2, v, mask)
    l1 = jnp.sum(st[0], axis=-1, keepdims=True)
    l2 = jnp.sum(st[2], axis=-1, keepdims=True)
    o = st[1] / l1 - lam * (st[3] / l2)
    o_ref[...] = _head_norm(o, g_ref[...]).astype(o_ref.dtype)


def _diff_attention(qkv, qkv_meta, lams, diff_norm_g, *, bq, bk):
    t = qkv.shape[0]
    mrows = qkv_meta.shape[0]
    qb, kb, vb = COL_DQ // HEAD_DIM, COL_DK // HEAD_DIM, COL_DV // (2 * HEAD_DIM)
    vec = pl.BlockSpec((1, HEAD_DIM), lambda h, i: (0, 0))
    return pl.pallas_call(
        functools.partial(_diff_attn_kernel, bq=bq, bk=bk),
        grid=(N_DIFF_HEADS, t // bq),
        in_specs=[
            vec, vec, vec, vec,
            pl.BlockSpec((1, 2 * HEAD_DIM), lambda h, i: (0, 0)),
            pl.BlockSpec((bq, HEAD_DIM), lambda h, i: (i, qb + 2 * h)),
            pl.BlockSpec((bq, HEAD_DIM), lambda h, i: (i, qb + 2 * h + 1)),
            pl.BlockSpec((t, HEAD_DIM), lambda h, i: (0, kb + 2 * h)),
            pl.BlockSpec((t, HEAD_DIM), lambda h, i: (0, kb + 2 * h + 1)),
            pl.BlockSpec((t, 2 * HEAD_DIM), lambda h, i: (0, vb + h)),
            pl.BlockSpec((mrows, HEAD_DIM), lambda h, i: (0, kb + 2 * h)),
            pl.BlockSpec((mrows, HEAD_DIM), lambda h, i: (0, kb + 2 * h + 1)),
            pl.BlockSpec((mrows, 2 * HEAD_DIM), lambda h, i: (0, vb + h)),
        ],
        out_specs=pl.BlockSpec((bq, 2 * HEAD_DIM), lambda h, i: (i, h)),
        out_shape=jax.ShapeDtypeStruct((t, DIFF_WIDTH), BF16),
        compiler_params=pltpu.CompilerParams(
            dimension_semantics=("arbitrary", "arbitrary"), vmem_limit_bytes=VMEM_LIMIT),
        name="diff_attn",
    )(*lams, diff_norm_g, qkv, qkv, qkv, qkv, qkv, qkv_meta, qkv_meta, qkv_meta)


def _layer_norm(x, g, b):
    mu = jnp.mean(x, axis=-1, keepdims=True)
    xc = x - mu
    var = jnp.mean(jnp.square(xc), axis=-1, keepdims=True)
    return (xc * lax.rsqrt(var + LN_EPS)) * g + b


def _merge_kernel(x_ref, osb_ref, od_ref, gsb_ref, gd_ref, wsb_ref, wd_ref, wo_ref, g_ref, b_ref, o_ref):
    br_sb = _dot(osb_ref[...], wsb_ref[...])
    br_d = _dot(od_ref[...], wd_ref[...])
    gated = gsb_ref[...].astype(F32) * br_sb + gd_ref[...].astype(F32) * br_d
    mix = _dot(gated.astype(BF16), wo_ref[...])
    o_ref[...] = _layer_norm(ALPHA * x_ref[...] + mix, g_ref[...], b_ref[...])


def _merge(x, o_sb, o_d, gates, w_sb, w_d, w_o, ln_g, ln_b, *, bm):
    m, d = x.shape
    row = lambda i: (i, 0)
    const = lambda i: (0, 0)
    return pl.pallas_call(
        _merge_kernel,
        grid=(m // bm,),
        in_specs=[
            pl.BlockSpec((bm, d), row),
            pl.BlockSpec((bm, SB_WIDTH), row),
            pl.BlockSpec((bm, DIFF_WIDTH), row),
            pl.BlockSpec((bm, d), row),
            pl.BlockSpec((bm, d), lambda i: (i, 1)),
            pl.BlockSpec((SB_WIDTH, d), const),
            pl.BlockSpec((DIFF_WIDTH, d), const),
            pl.BlockSpec((d, d), const),
            pl.BlockSpec((1, d), const),
            pl.BlockSpec((1, d), const),
        ],
        out_specs=pl.BlockSpec((bm, d), row),
        out_shape=jax.ShapeDtypeStruct((m, d), F32),
        compiler_params=pltpu.CompilerParams(
            dimension_semantics=("arbitrary",), vmem_limit_bytes=VMEM_LIMIT),
        name="merge",
    )(x, o_sb, o_d, gates, gates, w_sb, w_d, w_o, ln_g, ln_b)


def _ffn_kernel(x_ref, wu_ref, wd_ref, g_ref, b_ref, o_ref, xb_ref, acc_ref):
    f = pl.program_id(1)

    @pl.when(f == 0)
    def _():
        xb_ref[...] = x_ref[...].astype(BF16)
        acc_ref[...] = jnp.zeros_like(acc_ref)

    u = _dot(xb_ref[...], wu_ref[...])
    a = jnp.square(jnp.maximum(u, 0.0)).astype(BF16)
    acc_ref[...] += _dot(a, wd_ref[...])

    @pl.when(f == pl.num_programs(1) - 1)
    def _():
        o_ref[...] = _layer_norm(ALPHA * x_ref[...] + acc_ref[...], g_ref[...], b_ref[...])


def _ffn(x, w_up, w_down, ln_g, ln_b, *, bm, bf):
    m, d = x.shape
    dff = w_up.shape[1]
    return pl.pallas_call(
        _ffn_kernel,
        grid=(m // bm, dff // bf),
        in_specs=[
            pl.BlockSpec((bm, d), lambda i, f: (i, 0)),
            pl.BlockSpec((d, bf), lambda i, f: (0, f)),
            pl.BlockSpec((bf, d), lambda i, f: (f, 0)),
            pl.BlockSpec((1, d), lambda i, f: (0, 0)),
            pl.BlockSpec((1, d), lambda i, f: (0, 0)),
        ],
        out_specs=pl.BlockSpec((bm, d), lambda i, f: (i, 0)),
        out_shape=jax.ShapeDtypeStruct((m, d), F32),
        scratch_shapes=[pltpu.VMEM((bm, d), BF16), pltpu.VMEM((bm, d), F32)],
        compiler_params=pltpu.CompilerParams(
            dimension_semantics=("arbitrary", "arbitrary"), vmem_limit_bytes=VMEM_LIMIT),
        name="ffn",
    )(x, w_up, w_down, ln_g, ln_b)


N_ROWS = 8


def _sample_attn_kernel(pt_ref, lq1_ref, lk1_ref, lq2_ref, lk2_ref, g_ref, qsb_ref, qd_ref, kself_ref, vself_ref,
                        lt_ref, *refs, n_seq):
    sb_hbm, dc_refs = refs[0], refs[1:1 + n_seq]
    osb_ref, od_ref, c_ref, asb_ref, m_ref, l_ref, ad_ref, sb_buf, sb_sem, issued_ref = refs[1 + n_seq:]
    i, p = pl.program_id(0), pl.program_id(1)
    n_grp, n_pages = pl.num_programs(0), pl.num_programs(1)
    page = dc_refs[0].shape[0]
    n = page * N_ROWS
    n_col = n // LANES
    n_row = n_seq * N_ROWS
    half = N_ROWS // 2
    rows = lax.broadcasted_iota(jnp.int32, (n_row, n), 0) % N_ROWS
    lanes = lax.broadcasted_iota(jnp.int32, (n_row, n), 1)
    own = (lanes % N_ROWS) == rows
    slot = p % 2

    def page_matrix(ref, kv):
        return ref[:, kv].reshape(n, HEAD_DIM).astype(BF16)

    def sb_copy(grp, pg, slot_, u):
        start = pl.multiple_of(pt_ref[grp * n_seq + u, n_pages - 1 - pg] * page, page)
        return pltpu.make_async_copy(sb_hbm.at[pl.ds(start, page)], sb_buf.at[slot_, u], sb_sem.at[slot_, u])

    @pl.when(p == 0)
    def _():
        c_ref[...] = jnp.zeros_like(c_ref)
        asb_ref[...] = jnp.zeros_like(asb_ref)
        l_ref[...] = jnp.ones_like(l_ref)
        ad_ref[...] = vself_ref[...]
        qk = qd_ref[...].astype(F32) * kself_ref[...].astype(F32)
        s_self = jnp.sum(qk.reshape(n_row, HEAD_DIM), axis=-1, keepdims=True)
        m_ref[...] = jnp.broadcast_to(s_self, m_ref.shape)

    @pl.when((i == 0) & (p == 0))
    def _():
        for u in range(n_seq):
            sb_copy(0, 0, 0, u).start()
        issued_ref[0] = 1

    alive = jnp.max(c_ref[...]) > LOG_F32_ZERO
    last_page = p == n_pages - 1
    fetch_next = jnp.where(last_page, i + 1 < n_grp, alive)

    @pl.when(fetch_next)
    def _():
        grp = jnp.where(last_page, i + 1, i)
        pg = jnp.where(last_page, 0, p + 1)
        for u in range(n_seq):
            sb_copy(grp, pg, 1 - slot, u).start()

    issued_ref[1 - slot] = fetch_next.astype(jnp.int32)

    @pl.when(issued_ref[slot] == 1)
    def _():
        for u in range(n_seq):
            sb_copy(i, p, slot, u).wait()

    @pl.when(alive)
    def _():
        sbc_refs = [sb_buf.at[slot, u] for u in range(n_seq)]
        g = jnp.concatenate([_nt_dot(qsb_ref[u], page_matrix(sbc_refs[u], 0)) for u in range(n_seq)], axis=0)
        lk = jnp.where(own, -(jnp.maximum(g, 0.0) + jnp.log1p(jnp.exp(-jnp.abs(g)))), 0.0)
        lk_cols = jnp.concatenate([lk[:, j * LANES:(j + 1) * LANES] for j in range(n_col)], axis=0)
        s2 = _split_dot(lk_cols, lt_ref[...])
        c = c_ref[...]
        ws = [None] * n_col
        for j in reversed(range(n_col)):
            sl = slice(j * LANES, (j + 1) * LANES)
            e = g[:, sl] + lk[:, sl] + s2[j * n_row:(j + 1) * n_row, :LANES] + c
            ws[j] = jnp.where(own[:, sl], jnp.exp(e), 0.0)
            c = c + s2[j * n_row:(j + 1) * n_row, LANES:]
        c_ref[...] = jnp.where(own[:, :LANES], c, -jnp.inf)
        w = jnp.concatenate(ws, axis=1).astype(BF16)
        for u in range(n_seq):
            asb_ref[u] += _dot(w[u * N_ROWS:(u + 1) * N_ROWS], page_matrix(sbc_refs[u], 1))

    s = jnp.concatenate([_nt_dot(qd_ref[u], page_matrix(dc_refs[u], 0)) for u in range(n_seq)], axis=0)
    s = jnp.where(own, s, -jnp.inf)
    m = m_ref[...][:, :1]
    m_new = jnp.maximum(m, jnp.max(s, axis=-1, keepdims=True))
    a = jnp.exp(m - m_new)
    pr = jnp.exp(s - m_new)
    l_ref[...] = a * l_ref[...] + jnp.sum(pr, axis=-1, keepdims=True)
    m_ref[...] = jnp.broadcast_to(m_new, m_ref.shape)
    s_other = jnp.where(rows < half, pltpu.roll(s, half, 1), pltpu.roll(s, n - half, 1))
    pr_other = jnp.exp(s_other - m_new)
    for u in range(n_seq):
        sl = slice(u * N_ROWS, (u + 1) * N_ROWS)
        lhs = jnp.concatenate([pr[sl], pr_other[sl]], axis=0).astype(BF16)
        ad_ref[u] = jnp.concatenate([a[sl], a[sl]], axis=0) * ad_ref[u] + _dot(lhs, page_matrix(dc_refs[u], 1))

    @pl.when(p == pl.num_programs(1) - 1)
    def _():
        osb_ref[...] = asb_ref[...].astype(osb_ref.dtype)
        lam = _lambda(lq1_ref, lk1_ref, lq2_ref, lk2_ref)
        gain = g_ref[...]
        for u in range(n_seq):
            l = l_ref[u * N_ROWS:(u + 1) * N_ROWS, :1]
            x = ad_ref[u] / jnp.concatenate([l, l], axis=0)
            lo = x[0:half] - lam * x[3 * half:4 * half]
            hi = x[2 * half:3 * half] - lam * x[half:2 * half]
            ms = (jnp.sum(jnp.square(lo), axis=-1, keepdims=True)
                  + jnp.sum(jnp.square(hi), axis=-1, keepdims=True)) / (2 * HEAD_DIM)
            r = lax.rsqrt(ms + LN_EPS)
            od_ref[u] = jnp.concatenate([((lo * r) * gain[:, :HEAD_DIM]) * (1.0 - LAM_INIT),
                                         ((hi * r) * gain[:, HEAD_DIM:]) * (1.0 - LAM_INIT)],
                                        axis=1).astype(od_ref.dtype)


def _column_suffix_matrix():
    i = jnp.arange(LANES)
    same = (i[:, None] % N_ROWS) == (i[None, :] % N_ROWS)
    later = (i[:, None] // N_ROWS) > (i[None, :] // N_ROWS)
    return jnp.concatenate([same & later, same], axis=1).astype(BF16)


def _sample_attention(page_table, qkv_s, cache_sb, cache_d, lams, diff_norm_g, *, page, n_seq):
    b, n_pages = page_table.shape
    half = N_DIFF_HEADS

    def by_half(x):
        return x.reshape(b, N_DIFF_HEADS, 2, HEAD_DIM).transpose(0, 2, 1, 3).reshape(b, N_ROWS, HEAD_DIM)

    q_sb = qkv_s[:, COL_SBQ:COL_SBQ + SEG].reshape(b, N_ROWS, HEAD_DIM)
    q_d = by_half(qkv_s[:, COL_DQ:COL_DQ + SEG])
    k_self = by_half(qkv_s[:, COL_DK:COL_DK + SEG])
    v = by_half(qkv_s[:, COL_DV:COL_DV + SEG]).astype(F32)
    v_lo, v_hi = v[:, :half], v[:, half:]
    v_self = jnp.concatenate([v_lo, v_hi, v_hi, v_lo], axis=1)

    vec = pl.BlockSpec((1, HEAD_DIM), lambda i, p, pt: (0, 0))
    per_seq = lambda i, p, pt: (i, 0, 0)

    def page_spec(u):
        return pl.BlockSpec((page, 2, N_ROWS, HEAD_DIM),
                            lambda i, p, pt: (pt[i * n_seq + u, n_pages - 1 - p], 0, 0, 0))

    grid_spec = pltpu.PrefetchScalarGridSpec(
        num_scalar_prefetch=1,
        grid=(b // n_seq, n_pages),
        in_specs=[
            vec, vec, vec, vec,
            pl.BlockSpec((1, 2 * HEAD_DIM), lambda i, p, pt: (0, 0)),
            pl.BlockSpec((n_seq, N_ROWS, HEAD_DIM), per_seq),
            pl.BlockSpec((n_seq, N_ROWS, HEAD_DIM), per_seq),
            pl.BlockSpec((n_seq, N_ROWS, HEAD_DIM), per_seq),
            pl.BlockSpec((n_seq, 2 * N_ROWS, HEAD_DIM), per_seq),
            pl.BlockSpec((LANES, 2 * LANES), lambda i, p, pt: (0, 0)),
            pl.BlockSpec(memory_space=pl.ANY),
        ] + [page_spec(u) for u in range(n_seq)],
        out_specs=[
            pl.BlockSpec((n_seq, N_SB_HEADS, HEAD_DIM), per_seq),
            pl.BlockSpec((n_seq, N_DIFF_HEADS, 2 * HEAD_DIM), per_seq),
        ],
        scratch_shapes=[
            pltpu.VMEM((n_seq * N_ROWS, LANES), F32),
            pltpu.VMEM((n_seq, N_ROWS, HEAD_DIM), F32),
            pltpu.VMEM((n_seq * N_ROWS, LANES), F32),
            pltpu.VMEM((n_seq * N_ROWS, LANES), F32),
            pltpu.VMEM((n_seq, 2 * N_ROWS, HEAD_DIM), F32),
            pltpu.VMEM((2, n_seq, page, 2, N_ROWS, HEAD_DIM), F32),
            pltpu.SemaphoreType.DMA((2, n_seq)),
            pltpu.SMEM((2,), jnp.int32),
        ],
    )
    o_sb, o_d = pl.pallas_call(
        functools.partial(_sample_attn_kernel, n_seq=n_seq),
        grid_spec=grid_spec,
        out_shape=[
            jax.ShapeDtypeStruct((b, N_SB_HEADS, HEAD_DIM), BF16),
            jax.ShapeDtypeStruct((b, N_DIFF_HEADS, 2 * HEAD_DIM), BF16),
        ],
        compiler_params=pltpu.CompilerParams(
            dimension_semantics=("arbitrary", "arbitrary"), vmem_limit_bytes=VMEM_LIMIT),
        name="sample_attn",
    )(page_table, *lams, diff_norm_g, q_sb, q_d, k_self, v_self, _column_suffix_matrix(),
      cache_sb, *([cache_d] * n_seq))
    return o_sb.reshape(b, SB_WIDTH), o_d.reshape(b, DIFF_WIDTH)


def _suffix_matrix(n):
    j = jnp.arange(n)[:, None]
    s = jnp.arange(n)[None, :]
    return jnp.concatenate([(j > s), jnp.ones((n, n), bool)], axis=1).astype(BF16)


def _pick(m, candidates):
    for c in candidates:
        if m % c == 0:
            return c
    raise ValueError(f"no block size for {m}")


def kernel(x_prompt, x_sample, cache_sb_kv, cache_diff_kv, page_table, meta_tokens, w_in, lambda_q1, lambda_k1,
           lambda_q2, lambda_k2, diff_norm_g, w_branch_sb, w_branch_diff, w_out, ln_mix_g, ln_mix_b, w_up,
           w_down, ln_ffn_g, ln_ffn_b):
    assert w_in.shape[0] == DEPTH and x_prompt.shape[0] == 1 and x_sample.shape[1] == 1
    seq, d = x_prompt.shape[1:]
    n_dec = x_sample.shape[0]
    n_pool, page = cache_sb_kv.shape[1:3]
    past_len = page_table.shape[1] * page
    bq = _pick(seq, (256, 128))

    w_in_b = w_in[0].astype(BF16)
    w_sb_b, w_d_b, w_o_b = w_branch_sb[0].astype(BF16), w_branch_diff[0].astype(BF16), w_out[0].astype(BF16)
    w_up_b, w_down_b = w_up[0].astype(BF16), w_down[0].astype(BF16)
    lams = (lambda_q1, lambda_k1, lambda_q2, lambda_k2)

    xp = x_prompt[0]
    rope_p = _rope_table(N_META + jnp.arange(seq))
    qkv_p, sbkv_p, dkv_p, gates_p = _project(xp, w_in_b, rope_p, bm=_pick(seq, (1024, 512, 256, 128)), bn=512)
    x_small = jnp.concatenate([meta_tokens.astype(F32), x_sample[:, 0]], axis=0)
    pos_small = jnp.concatenate([jnp.arange(N_META), jnp.full((n_dec,), past_len)])
    qkv_s, sbkv_s, dkv_s, gates_s = _project(x_small, w_in_b, _rope_table(pos_small), bm=N_META + n_dec, bn=512)

    tri = _suffix_matrix(bq)
    qkv_meta = jnp.concatenate([qkv_s[:N_META], jnp.zeros((bq - N_META, QKV_COLS), BF16)], axis=0)
    o_sb = _sb_attention(qkv_p, qkv_meta, tri, bq=bq, n_heads=4)
    bqd = _pick(seq, (1024, 512, 256, 128))
    o_d = _diff_attention(qkv_p, qkv_meta, lams, diff_norm_g, bq=bqd, bk=bqd)
    x1 = _merge(xp, o_sb, o_d, gates_p, w_sb_b, w_d_b, w_o_b, ln_mix_g, ln_mix_b, bm=_pick(seq, (256, 128)))
    y_prompt = _ffn(x1, w_up_b, w_down_b, ln_ffn_g, ln_ffn_b, bm=_pick(seq, (512, 256, 128)), bf=1024)

    cache_sb = cache_sb_kv.reshape(n_pool * page, 2, N_SB_HEADS, HEAD_DIM)
    cache_d = cache_diff_kv.reshape(n_pool * page, 2, N_DIFF_HEADS, 2, HEAD_DIM).transpose(0, 1, 3, 2, 4).reshape(
        n_pool * page, 2, N_ROWS, HEAD_DIM)
    os_sb, os_d = _sample_attention(page_table, qkv_s[N_META:], cache_sb, cache_d, lams, diff_norm_g, page=page,
                                    n_seq=_pick(n_dec, (8, 4, 2, 1)))
    xs1 = _merge(x_sample[:, 0], os_sb, os_d, gates_s[N_META:], w_sb_b, w_d_b, w_o_b, ln_mix_g, ln_mix_b,
                 bm=n_dec)
    y_sample = _ffn(xs1, w_up_b, w_down_b, ln_ffn_g, ln_ffn_b, bm=n_dec, bf=512)

    t_p = N_META + seq
    sb_kv_prompt = jnp.concatenate([sbkv_s[:N_META], sbkv_p], axis=0).reshape(1, 1, t_p, 2, N_SB_HEADS, HEAD_DIM)
    diff_kv_prompt = jnp.concatenate([dkv_s[:N_META], dkv_p], axis=0).reshape(
        1, 1, t_p, 2, N_DIFF_HEADS, 2 * HEAD_DIM)
    sb_kv_sample = sbkv_s[N_META:].reshape(1, n_dec, 1, 2, N_SB_HEADS, HEAD_DIM)
    diff_kv_sample = dkv_s[N_META:].reshape(1, n_dec, 1, 2, N_DIFF_HEADS, 2 * HEAD_DIM)
    return (y_prompt[None], y_sample[:, None], sb_kv_prompt, diff_kv_prompt, sb_kv_sample, diff_kv_sample)
```

```python
import functools
import math

import jax
import jax.numpy as jnp
from jax import lax
from jax.experimental import pallas as pl
from jax.experimental.pallas import tpu as pltpu

F32 = jnp.float32
BF16 = jnp.bfloat16

HEAD_DIM = 128
N_SB_HEADS = 8
N_DIFF_HEADS = 4
SB_WIDTH = N_SB_HEADS * HEAD_DIM
DIFF_WIDTH = N_DIFF_HEADS * 2 * HEAD_DIM
N_META = 16
ROT_DIM = HEAD_DIM // 4
ROPE_THETA = 500000.0
LN_EPS = 1e-5
DEPTH = 1
ALPHA = (2 * DEPTH) ** 0.25
LAM_INIT = 0.8 - 0.6 * math.exp(-0.3 * 0)
QK_SCALE = HEAD_DIM ** -0.5
LANES = 128
VMEM_LIMIT = 56 * 1024 * 1024

COL_SBQ, COL_SBK, COL_SBV = 0, SB_WIDTH, 2 * SB_WIDTH
COL_DQ, COL_DK, COL_DV = 3 * SB_WIDTH, 3 * SB_WIDTH + DIFF_WIDTH, 3 * SB_WIDTH + 2 * DIFF_WIDTH
QKV_COLS = 3 * SB_WIDTH + 3 * DIFF_WIDTH
SEG = 1024


def _nt_dot(a, b):
    return lax.dot_general(a, b, (((1,), (1,)), ((), ())), preferred_element_type=F32)


def _dot(a, b):
    return jnp.dot(a, b, preferred_element_type=F32)


def _rope_table(pos):
    half = ROT_DIM // 2
    inv = ROPE_THETA ** (-jnp.arange(half, dtype=F32) / half)
    ang = pos.astype(F32)[:, None] * inv[None, :]
    cos, sin = jnp.cos(ang), jnp.sin(ang)
    t = pos.shape[0]
    rest1 = jnp.ones((t, HEAD_DIM - ROT_DIM), F32)
    rest0 = jnp.zeros((t, HEAD_DIM - ROT_DIM), F32)
    z = jnp.zeros((t, half), F32)
    c = jnp.concatenate([cos, cos, rest1], axis=1)
    s1 = jnp.concatenate([-sin, z, rest0], axis=1)
    s2 = jnp.concatenate([z, sin, rest0], axis=1)
    return jnp.concatenate([c, s1, s2], axis=1)


def _rope(z, rope):
    c, s1, s2 = rope[:, :LANES], rope[:, LANES:2 * LANES], rope[:, 2 * LANES:]
    half = ROT_DIM // 2
    outs = []
    for i in range(z.shape[1] // LANES):
        zc = z[:, i * LANES:(i + 1) * LANES]
        outs.append(zc * c + pltpu.roll(zc, LANES - half, 1) * s1 + pltpu.roll(zc, half, 1) * s2)
    return jnp.concatenate(outs, axis=1)


def _proj_kernel(x_ref, w_ref, rope_ref, qkv_ref, sbkv_ref, dkv_ref, gate_ref, xb_ref, *, nb):
    j = pl.program_id(1)

    @pl.when(j == 0)
    def _():
        xb_ref[...] = x_ref[...].astype(BF16)

    z = _dot(xb_ref[...], w_ref[...])
    seg = j // nb

    @pl.when(seg == 0)
    def _():
        qkv_ref[...] = (z * QK_SCALE).astype(BF16)

    @pl.when((seg == 1) | (seg == 2))
    def _():
        qkv_ref[...] = z.astype(BF16)
        sbkv_ref[...] = z

    @pl.when(seg == 3)
    def _():
        qkv_ref[...] = (_rope(z, rope_ref[...]) * QK_SCALE).astype(BF16)

    @pl.when(seg == 4)
    def _():
        r = _rope(z, rope_ref[...])
        qkv_ref[...] = r.astype(BF16)
        dkv_ref[...] = r

    @pl.when(seg == 5)
    def _():
        qkv_ref[...] = z.astype(BF16)
        dkv_ref[...] = z

    @pl.when(seg >= 6)
    def _():
        gate_ref[...] = (0.5 * jnp.tanh(0.5 * z) + 0.5).astype(BF16)


def _project(x, w_bf16, rope, *, bm, bn):
    m, d = x.shape
    n = w_bf16.shape[1]
    nb = SEG // bn
    grid = (m // bm, n // bn)
    return pl.pallas_call(
        functools.partial(_proj_kernel, nb=nb),
        grid=grid,
        in_specs=[
            pl.BlockSpec((bm, d), lambda i, j: (i, 0)),
            pl.BlockSpec((d, bn), lambda i, j: (0, j)),
            pl.BlockSpec((bm, 3 * LANES), lambda i, j: (i, 0)),
        ],
        out_specs=[
            pl.BlockSpec((bm, bn), lambda i, j: (i, jnp.minimum(j, 6 * nb - 1))),
            pl.BlockSpec((bm, bn), lambda i, j: (i, jnp.clip(j - nb, 0, 2 * nb - 1))),
            pl.BlockSpec((bm, bn), lambda i, j: (i, jnp.clip(j - 4 * nb, 0, 2 * nb - 1))),
            pl.BlockSpec((bm, bn), lambda i, j: (i, jnp.clip(j - 6 * nb, 0, 4 * nb - 1))),
        ],
        out_shape=[
            jax.ShapeDtypeStruct((m, QKV_COLS), BF16),
            jax.ShapeDtypeStruct((m, 2 * SB_WIDTH), F32),
            jax.ShapeDtypeStruct((m, 2 * DIFF_WIDTH), F32),
            jax.ShapeDtypeStruct((m, n - QKV_COLS), BF16),
        ],
        scratch_shapes=[pltpu.VMEM((bm, d), BF16)],
        compiler_params=pltpu.CompilerParams(
            dimension_semantics=("arbitrary", "arbitrary"), vmem_limit_bytes=VMEM_LIMIT),
        name="proj",
    )(x, w_bf16, rope)


def _split_dot(x, t):
    hi = x.astype(BF16)
    r1 = x - hi.astype(F32)
    mid = r1.astype(BF16)
    lo = (r1 - mid.astype(F32)).astype(BF16)
    return _dot(hi, t) + _dot(mid, t) + _dot(lo, t)


LOG_F32_ZERO = -104.0


def _sb_attn_kernel(q_ref, k_ref, v_ref, km_ref, vm_ref, tri_ref, o_ref, c_ref, acc_ref, *, bq, n_heads):
    qb = pl.program_id(1)
    tri = tri_ref[...]
    row = lax.broadcasted_iota(jnp.int32, (n_heads * bq, bq), 0) % bq
    col = lax.broadcasted_iota(jnp.int32, (n_heads * bq, bq), 1)
    head_cols = [slice(g * HEAD_DIM, (g + 1) * HEAD_DIM) for g in range(n_heads)]
    qs = [q_ref[:, hc] for hc in head_cols]

    def block(kv_rows, k_ref, v_ref, mask, c, acc):
        z = jnp.concatenate([_nt_dot(q, k_ref[kv_rows, hc]) for q, hc in zip(qs, head_cols)], axis=0)
        lk = -(jnp.maximum(z, 0.0) + jnp.log1p(jnp.exp(-jnp.abs(z))))
        if mask is not None:
            lk = jnp.where(mask, lk, 0.0)
        s2 = _split_dot(lk, tri)
        w = jnp.exp(z + lk + s2[:, :bq] + c)
        if mask is not None:
            w = jnp.where(mask, w, 0.0)
        w = w.astype(BF16)
        pv = [_dot(w[g * bq:(g + 1) * bq], v_ref[kv_rows, hc]) for g, hc in enumerate(head_cols)]
        return c + s2[:, bq:], acc + jnp.concatenate(pv, axis=0)

    start = pl.multiple_of(qb * bq, bq)
    c, acc = block(pl.ds(start, bq), k_ref, v_ref, col < row,
                   jnp.zeros((n_heads * bq, bq), F32), jnp.zeros((n_heads * bq, HEAD_DIM), F32))
    c_ref[...] = c
    acc_ref[...] = acc

    def live(carry):
        kb, c_max = carry
        return (kb >= 0) & (c_max > LOG_F32_ZERO)

    def body(carry):
        kb, _ = carry
        s = pl.multiple_of(kb * bq, bq)
        c, acc = block(pl.ds(s, bq), k_ref, v_ref, None, c_ref[...], acc_ref[...])
        c_ref[...] = c
        acc_ref[...] = acc
        return kb - 1, jnp.max(c)

    _, c_max = lax.while_loop(live, body, (qb - 1, jnp.max(c)))

    @pl.when(c_max > LOG_F32_ZERO)
    def _():
        _, acc = block(slice(None), km_ref, vm_ref, col < N_META, c_ref[...], acc_ref[...])
        acc_ref[...] = acc

    for g, hc in enumerate(head_cols):
        o_ref[:, hc] = acc_ref[g * bq:(g + 1) * bq, :].astype(o_ref.dtype)


def _sb_attention(qkv, qkv_meta, tri, *, bq, n_heads):
    t = qkv.shape[0]
    w = n_heads * HEAD_DIM
    kb, vb = COL_SBK // w, COL_SBV // w
    return pl.pallas_call(
        functools.partial(_sb_attn_kernel, bq=bq, n_heads=n_heads),
        grid=(N_SB_HEADS // n_heads, t // bq),
        in_specs=[
            pl.BlockSpec((bq, w), lambda h, i: (i, h)),
            pl.BlockSpec((t, w), lambda h, i: (0, kb + h)),
            pl.BlockSpec((t, w), lambda h, i: (0, vb + h)),
            pl.BlockSpec((bq, w), lambda h, i: (0, kb + h)),
            pl.BlockSpec((bq, w), lambda h, i: (0, vb + h)),
            pl.BlockSpec((bq, 2 * bq), lambda h, i: (0, 0)),
        ],
        out_specs=pl.BlockSpec((bq, w), lambda h, i: (i, h)),
        out_shape=jax.ShapeDtypeStruct((t, SB_WIDTH), BF16),
        scratch_shapes=[pltpu.VMEM((n_heads * bq, bq), F32), pltpu.VMEM((n_heads * bq, HEAD_DIM), F32)],
        compiler_params=pltpu.CompilerParams(
            dimension_semantics=("arbitrary", "arbitrary"), vmem_limit_bytes=VMEM_LIMIT),
        name="sb_attn",
    )(qkv, qkv, qkv, qkv_meta, qkv_meta, tri)


def _lambda(lq1_ref, lk1_ref, lq2_ref, lk2_ref):
    a = jnp.sum(lq1_ref[...] * lk1_ref[...], axis=-1, keepdims=True)
    b = jnp.sum(lq2_ref[...] * lk2_ref[...], axis=-1, keepdims=True)
    return jnp.exp(a) - jnp.exp(b) + LAM_INIT


def _head_norm(o, g):
    of = o * lax.rsqrt(jnp.mean(jnp.square(o), axis=-1, keepdims=True) + LN_EPS)
    return (of * g) * (1.0 - LAM_INIT)


def _diff_attn_kernel(lq1_ref, lk1_ref, lq2_ref, lk2_ref, g_ref, q1_ref, q2_ref, k1_ref, k2_ref, v_ref,
                      km1_ref, km2_ref, vm_ref, o_ref, *, bq, bk):
    qb = pl.program_id(1)
    lam = _lambda(lq1_ref, lk1_ref, lq2_ref, lk2_ref)
    q1, q2 = q1_ref[...], q2_ref[...]
    n_meta_rows = km1_ref.shape[0]
    n_full = qb * (bq // bk)
    meta_mask = lax.broadcasted_iota(jnp.int32, (bq, n_meta_rows), 1) < N_META
    row = lax.broadcasted_iota(jnp.int32, (bq, bk), 0)
    col = lax.broadcasted_iota(jnp.int32, (bq, bk), 1)

    def scores(k1, k2, mask):
        s1, s2 = _nt_dot(q1, k1), _nt_dot(q2, k2)
        if mask is not None:
            s1, s2 = jnp.where(mask, s1, -jnp.inf), jnp.where(mask, s2, -jnp.inf)
        return s1, s2

    def lanewise(op, x):
        out = x[:, :LANES]
        for j in range(1, x.shape[1] // LANES):
            out = op(out, x[:, j * LANES:(j + 1) * LANES])
        return out

    def key_block(start):
        s = pl.multiple_of(start, bk)
        return k1_ref[pl.ds(s, bk), :], k2_ref[pl.ds(s, bk), :], v_ref[pl.ds(s, bk), :]

    def diag_blocks():
        for j in range(bq // bk):
            yield key_block(qb * bq + j * bk) + (col + j * bk <= row,)

    def max_update(mx, k1, k2, mask):
        s1, s2 = scores(k1, k2, mask)
        return jnp.maximum(mx[0], lanewise(jnp.maximum, s1)), jnp.maximum(mx[1], lanewise(jnp.maximum, s2))

    neg = jnp.full((bq, LANES), -jnp.inf, F32)
    mx = max_update((neg, neg), km1_ref[...], km2_ref[...], meta_mask)
    mx = lax.fori_loop(0, n_full, lambda i, mx: max_update(mx, *key_block(i * bk)[:2], None), mx)
    for k1, k2, _, mask in diag_blocks():
        mx = max_update(mx, k1, k2, mask)
    m1 = jnp.max(mx[0], axis=-1, keepdims=True)
    m2 = jnp.max(mx[1], axis=-1, keepdims=True)

    def acc_update(st, k1, k2, v, mask):
        s1, s2 = scores(k1, k2, mask)
        p1, p2 = jnp.exp(s1 - m1), jnp.exp(s2 - m2)
        return (st[0] + lanewise(jnp.add, p1), st[1] + _dot(p1.astype(BF16), v),
                st[2] + lanewise(jnp.add, p2), st[3] + _dot(p2.astype(BF16), v))

    zl, za = jnp.zeros((bq, LANES), F32), jnp.zeros((bq, 2 * HEAD_DIM), F32)
    st = acc_update((zl, za, zl, za), km1_ref[...], km2_ref[...], vm_ref[...], meta_mask)
    st = lax.fori_loop(0, n_full, lambda i, st: acc_update(st, *key_block(i * bk), None), st)
    for k1, k2, v, mask in diag_blocks():
        st = acc_update(st, k1, k2, v, mask)
    l1 = jnp.sum(st[0], axis=-1, keepdims=True)
    l2 = jnp.sum(st[2], axis=-1, keepdims=True)
    o = st[1] / l1 - lam * (st[3] / l2)
    o_ref[...] = _head_norm(o, g_ref[...]).astype(o_ref.dtype)


def _diff_attention(qkv, qkv_meta, lams, diff_norm_g, *, bq, bk):
    t = qkv.shape[0]
    mrows = qkv_meta.shape[0]
    qb, kb, vb = COL_DQ // HEAD_DIM, COL_DK // HEAD_DIM, COL_DV // (2 * HEAD_DIM)
    vec = pl.BlockSpec((1, HEAD_DIM), lambda h, i: (0, 0))
    return pl.pallas_call(
        functools.partial(_diff_attn_kernel, bq=bq, bk=bk),
        grid=(N_DIFF_HEADS, t // bq),
        in_specs=[
            vec, vec, vec, vec,
            pl.BlockSpec((1, 2 * HEAD_DIM), lambda h, i: (0, 0)),
            pl.BlockSpec((bq, HEAD_DIM), lambda h, i: (i, qb + 2 * h)),
            pl.BlockSpec((bq, HEAD_DIM), lambda h, i: (i, qb + 2 * h + 1)),
            pl.BlockSpec((t, HEAD_DIM), lambda h, i: (0, kb + 2 * h)),
            pl.BlockSpec((t, HEAD_DIM), lambda h, i: (0, kb + 2 * h + 1)),
            pl.BlockSpec((t, 2 * HEAD_DIM), lambda h, i: (0, vb + h)),
            pl.BlockSpec((mrows, HEAD_DIM), lambda h, i: (0, kb + 2 * h)),
            pl.BlockSpec((mrows, HEAD_DIM), lambda h, i: (0, kb + 2 * h + 1)),
            pl.BlockSpec((mrows, 2 * HEAD_DIM), lambda h, i: (0, vb + h)),
        ],
        out_specs=pl.BlockSpec((bq, 2 * HEAD_DIM), lambda h, i: (i, h)),
        out_shape=jax.ShapeDtypeStruct((t, DIFF_WIDTH), BF16),
        compiler_params=pltpu.CompilerParams(
            dimension_semantics=("arbitrary", "arbitrary"), vmem_limit_bytes=VMEM_LIMIT),
        name="diff_attn",
    )(*lams, diff_norm_g, qkv, qkv, qkv, qkv, qkv, qkv_meta, qkv_meta, qkv_meta)


def _layer_norm(x, g, b):
    mu = jnp.mean(x, axis=-1, keepdims=True)
    xc = x - mu
    var = jnp.mean(jnp.square(xc), axis=-1, keepdims=True)
    return (xc * lax.rsqrt(var + LN_EPS)) * g + b


def _merge_kernel(x_ref, osb_ref, od_ref, gsb_ref, gd_ref, wsb_ref, wd_ref, wo_ref, g_ref, b_ref, o_ref):
    br_sb = _dot(osb_ref[...], wsb_ref[...])
    br_d = _dot(od_ref[...], wd_ref[...])
    gated = gsb_ref[...].astype(F32) * br_sb + gd_ref[...].astype(F32) * br_d
    mix = _dot(gated.astype(BF16), wo_ref[...])
    o_ref[...] = _layer_norm(ALPHA * x_ref[...] + mix, g_ref[...], b_ref[...])


def _merge(x, o_sb, o_d, gates, w_sb, w_d, w_o, ln_g, ln_b, *, bm):
    m, d = x.shape
    row = lambda i: (i, 0)
    const = lambda i: (0, 0)
    return pl.pallas_call(
        _merge_kernel,
        grid=(m // bm,),
        in_specs=[
            pl.BlockSpec((bm, d), row),
            pl.BlockSpec((bm, SB_WIDTH), row),
            pl.BlockSpec((bm, DIFF_WIDTH), row),
            pl.BlockSpec((bm, d), row),
            pl.BlockSpec((bm, d), lambda i: (i, 1)),
            pl.BlockSpec((SB_WIDTH, d), const),
            pl.BlockSpec((DIFF_WIDTH, d), const),
            pl.BlockSpec((d, d), const),
            pl.BlockSpec((1, d), const),
            pl.BlockSpec((1, d), const),
        ],
        out_specs=pl.BlockSpec((bm, d), row),
        out_shape=jax.ShapeDtypeStruct((m, d), F32),
        compiler_params=pltpu.CompilerParams(
            dimension_semantics=("arbitrary",), vmem_limit_bytes=VMEM_LIMIT),
        name="merge",
    )(x, o_sb, o_d, gates, gates, w_sb, w_d, w_o, ln_g, ln_b)


def _ffn_kernel(x_ref, wu_ref, wd_ref, g_ref, b_ref, o_ref, xb_ref, acc_ref):
    f = pl.program_id(1)

    @pl.when(f == 0)
    def _():
        xb_ref[...] = x_ref[...].astype(BF16)
        acc_ref[...] = jnp.zeros_like(acc_ref)

    u = _dot(xb_ref[...], wu_ref[...])
    a = jnp.square(jnp.maximum(u, 0.0)).astype(BF16)
    acc_ref[...] += _dot(a, wd_ref[...])

    @pl.when(f == pl.num_programs(1) - 1)
    def _():
        o_ref[...] = _layer_norm(ALPHA * x_ref[...] + acc_ref[...], g_ref[...], b_ref[...])


def _ffn(x, w_up, w_down, ln_g, ln_b, *, bm, bf):
    m, d = x.shape
    dff = w_up.shape[1]
    return pl.pallas_call(
        _ffn_kernel,
        grid=(m // bm, dff // bf),
        in_specs=[
            pl.BlockSpec((bm, d), lambda i, f: (i, 0)),
            pl.BlockSpec((d, bf), lambda i, f: (0, f)),
            pl.BlockSpec((bf, d), lambda i, f: (f, 0)),
            pl.BlockSpec((1, d), lambda i, f: (0, 0)),
            pl.BlockSpec((1, d), lambda i, f: (0, 0)),
        ],
        out_specs=pl.BlockSpec((bm, d), lambda i, f: (i, 0)),
        out_shape=jax.ShapeDtypeStruct((m, d), F32),
        scratch_shapes=[pltpu.VMEM((bm, d), BF16), pltpu.VMEM((bm, d), F32)],
        compiler_params=pltpu.CompilerParams(
            dimension_semantics=("arbitrary", "arbitrary"), vmem_limit_bytes=VMEM_LIMIT),
        name="ffn",
    )(x, w_up, w_down, ln_g, ln_b)


N_ROWS = 8


def _sample_attn_kernel(pt_ref, lq1_ref, lk1_ref, lq2_ref, lk2_ref, g_ref, qsb_ref, qd_ref, kself_ref, vself_ref,
                        lt_ref, *refs, n_seq):
    sb_hbm, dc_refs = refs[0], refs[1:1 + n_seq]
    osb_ref, od_ref, c_ref, asb_ref, m_ref, l_ref, ad_ref, sb_buf, sb_sem, issued_ref = refs[1 + n_seq:]
    i, p = pl.program_id(0), pl.program_id(1)
    n_grp, n_pages = pl.num_programs(0), pl.num_programs(1)
    page = dc_refs[0].shape[0]
    n = page * N_ROWS
    n_col = n // LANES
    n_row = n_seq * N_ROWS
    half = N_ROWS // 2
    rows = lax.broadcasted_iota(jnp.int32, (n_row, n), 0) % N_ROWS
    lanes = lax.broadcasted_iota(jnp.int32, (n_row, n), 1)
    own = (lanes % N_ROWS) == rows
    slot = p % 2

    def page_matrix(ref, kv):
        return ref[:, kv].reshape(n, HEAD_DIM).astype(BF16)

    def sb_copy(grp, pg, slot_, u):
        start = pl.multiple_of(pt_ref[grp * n_seq + u, n_pages - 1 - pg] * page, page)
        return pltpu.make_async_copy(sb_hbm.at[pl.ds(start, page)], sb_buf.at[slot_, u], sb_sem.at[slot_, u])

    @pl.when(p == 0)
    def _():
        c_ref[...] = jnp.zeros_like(c_ref)
        asb_ref[...] = jnp.zeros_like(asb_ref)
        l_ref[...] = jnp.ones_like(l_ref)
        ad_ref[...] = vself_ref[...]
        qk = qd_ref[...].astype(F32) * kself_ref[...].astype(F32)
        s_self = jnp.sum(qk.reshape(n_row, HEAD_DIM), axis=-1, keepdims=True)
        m_ref[...] = jnp.broadcast_to(s_self, m_ref.shape)

    @pl.when((i == 0) & (p == 0))
    def _():
        for u in range(n_seq):
            sb_copy(0, 0, 0, u).start()
        issued_ref[0] = 1

    alive = jnp.max(c_ref[...]) > LOG_F32_ZERO
    last_page = p == n_pages - 1
    fetch_ahead = jnp.where(last_page, i + 1 < n_grp, p == 0)

    @pl.when(fetch_ahead)
    def _():
        grp = jnp.where(last_page, i + 1, i)
        pg = jnp.where(last_page, 0, p + 1)
        for u in range(n_seq):
            sb_copy(grp, pg, 1 - slot, u).start()

    issued_ref[1 - slot] = fetch_ahead.astype(jnp.int32)

    @pl.when(issued_ref[slot] == 1)
    def _():
        for u in range(n_seq):
            sb_copy(i, p, slot, u).wait()

    @pl.when(alive)
    def _():
        sbc_refs = [sb_buf.at[slot, u] for u in range(n_seq)]
        g = jnp.concatenate([_nt_dot(qsb_ref[u], page_matrix(sbc_refs[u], 0)) for u in range(n_seq)], axis=0)
        lk = jnp.where(own, -(jnp.maximum(g, 0.0) + jnp.log1p(jnp.exp(-jnp.abs(g)))), 0.0)
        lk_cols = jnp.concatenate([lk[:, j * LANES:(j + 1) * LANES] for j in range(n_col)], axis=0)
        s2 = _split_dot(lk_cols, lt_ref[...])
        c = c_ref[...]
        ws = [None] * n_col
        for j in reversed(range(n_col)):
            sl = slice(j * LANES, (j + 1) * LANES)
            e = g[:, sl] + lk[:, sl] + s2[j * n_row:(j + 1) * n_row, :LANES] + c
            ws[j] = jnp.where(own[:, sl], jnp.exp(e), 0.0)
            c = c + s2[j * n_row:(j + 1) * n_row, LANES:]
        c_ref[...] = jnp.where(own[:, :LANES], c, -jnp.inf)
        w = jnp.concatenate(ws, axis=1).astype(BF16)
        for u in range(n_seq):
            asb_ref[u] += _dot(w[u * N_ROWS:(u + 1) * N_ROWS], page_matrix(sbc_refs[u], 1))

    @pl.when((p > 0) & jnp.logical_not(last_page) & (jnp.max(c_ref[...]) > LOG_F32_ZERO))
    def _():
        for u in range(n_seq):
            sb_copy(i, p + 1, 1 - slot, u).start()
        issued_ref[1 - slot] = 1

    s = jnp.concatenate([_nt_dot(qd_ref[u], page_matrix(dc_refs[u], 0)) for u in range(n_seq)], axis=0)
    s = jnp.where(own, s, -jnp.inf)
    m = m_ref[...][:, :1]
    m_new = jnp.maximum(m, jnp.max(s, axis=-1, keepdims=True))
    a = jnp.exp(m - m_new)
    pr = jnp.exp(s - m_new)
    l_ref[...] = a * l_ref[...] + jnp.sum(pr, axis=-1, keepdims=True)
    m_ref[...] = jnp.broadcast_to(m_new, m_ref.shape)
    s_other = jnp.where(rows < half, pltpu.roll(s, half, 1), pltpu.roll(s, n - half, 1))
    pr_other = jnp.exp(s_other - m_new)
    for u in range(n_seq):
        sl = slice(u * N_ROWS, (u + 1) * N_ROWS)
        lhs = jnp.concatenate([pr[sl], pr_other[sl]], axis=0).astype(BF16)
        ad_ref[u] = jnp.concatenate([a[sl], a[sl]], axis=0) * ad_ref[u] + _dot(lhs, page_matrix(dc_refs[u], 1))

    @pl.when(p == pl.num_programs(1) - 1)
    def _():
        osb_ref[...] = asb_ref[...].astype(osb_ref.dtype)
        lam = _lambda(lq1_ref, lk1_ref, lq2_ref, lk2_ref)
        gain = g_ref[...]
        for u in range(n_seq):
            l = l_ref[u * N_ROWS:(u + 1) * N_ROWS, :1]
            x = ad_ref[u] / jnp.concatenate([l, l], axis=0)
            lo = x[0:half] - lam * x[3 * half:4 * half]
            hi = x[2 * half:3 * half] - lam * x[half:2 * half]
            ms = (jnp.sum(jnp.square(lo), axis=-1, keepdims=True)
                  + jnp.sum(jnp.square(hi), axis=-1, keepdims=True)) / (2 * HEAD_DIM)
            r = lax.rsqrt(ms + LN_EPS)
            od_ref[u] = jnp.concatenate([((lo * r) * gain[:, :HEAD_DIM]) * (1.0 - LAM_INIT),
                                         ((hi * r) * gain[:, HEAD_DIM:]) * (1.0 - LAM_INIT)],
                                        axis=1).astype(od_ref.dtype)


def _column_suffix_matrix():
    i = jnp.arange(LANES)
    same = (i[:, None] % N_ROWS) == (i[None, :] % N_ROWS)
    later = (i[:, None] // N_ROWS) > (i[None, :] // N_ROWS)
    return jnp.concatenate([same & later, same], axis=1).astype(BF16)


def _sample_attention(page_table, qkv_s, cache_sb, cache_d, lams, diff_norm_g, *, page, n_seq):
    b, n_pages = page_table.shape
    half = N_DIFF_HEADS

    def by_half(x):
        return x.reshape(b, N_DIFF_HEADS, 2, HEAD_DIM).transpose(0, 2, 1, 3).reshape(b, N_ROWS, HEAD_DIM)

    q_sb = qkv_s[:, COL_SBQ:COL_SBQ + SEG].reshape(b, N_ROWS, HEAD_DIM)
    q_d = by_half(qkv_s[:, COL_DQ:COL_DQ + SEG])
    k_self = by_half(qkv_s[:, COL_DK:COL_DK + SEG])
    v = by_half(qkv_s[:, COL_DV:COL_DV + SEG]).astype(F32)
    v_lo, v_hi = v[:, :half], v[:, half:]
    v_self = jnp.concatenate([v_lo, v_hi, v_hi, v_lo], axis=1)

    vec = pl.BlockSpec((1, HEAD_DIM), lambda i, p, pt: (0, 0))
    per_seq = lambda i, p, pt: (i, 0, 0)

    def page_spec(u):
        return pl.BlockSpec((page, 2, N_ROWS, HEAD_DIM),
                            lambda i, p, pt: (pt[i * n_seq + u, n_pages - 1 - p], 0, 0, 0))

    grid_spec = pltpu.PrefetchScalarGridSpec(
        num_scalar_prefetch=1,
        grid=(b // n_seq, n_pages),
        in_specs=[
            vec, vec, vec, vec,
            pl.BlockSpec((1, 2 * HEAD_DIM), lambda i, p, pt: (0, 0)),
            pl.BlockSpec((n_seq, N_ROWS, HEAD_DIM), per_seq),
            pl.BlockSpec((n_seq, N_ROWS, HEAD_DIM), per_seq),
            pl.BlockSpec((n_seq, N_ROWS, HEAD_DIM), per_seq),
            pl.BlockSpec((n_seq, 2 * N_ROWS, HEAD_DIM), per_seq),
            pl.BlockSpec((LANES, 2 * LANES), lambda i, p, pt: (0, 0)),
            pl.BlockSpec(memory_space=pl.ANY),
        ] + [page_spec(u) for u in range(n_seq)],
        out_specs=[
            pl.BlockSpec((n_seq, N_SB_HEADS, HEAD_DIM), per_seq),
            pl.BlockSpec((n_seq, N_DIFF_HEADS, 2 * HEAD_DIM), per_seq),
        ],
        scratch_shapes=[
            pltpu.VMEM((n_seq * N_ROWS, LANES), F32),
            pltpu.VMEM((n_seq, N_ROWS, HEAD_DIM), F32),
            pltpu.VMEM((n_seq * N_ROWS, LANES), F32),
            pltpu.VMEM((n_seq * N_ROWS, LANES), F32),
            pltpu.VMEM((n_seq, 2 * N_ROWS, HEAD_DIM), F32),
            pltpu.VMEM((2, n_seq, page, 2, N_ROWS, HEAD_DIM), F32),
            pltpu.SemaphoreType.DMA((2, n_seq)),
            pltpu.SMEM((2,), jnp.int32),
        ],
    )
    o_sb, o_d = pl.pallas_call(
        functools.partial(_sample_attn_kernel, n_seq=n_seq),
        grid_spec=grid_spec,
        out_shape=[
            jax.ShapeDtypeStruct((b, N_SB_HEADS, HEAD_DIM), BF16),
            jax.ShapeDtypeStruct((b, N_DIFF_HEADS, 2 * HEAD_DIM), BF16),
        ],
        compiler_params=pltpu.CompilerParams(
            dimension_semantics=("arbitrary", "arbitrary"), vmem_limit_bytes=VMEM_LIMIT),
        name="sample_attn",
    )(page_table, *lams, diff_norm_g, q_sb, q_d, k_self, v_self, _column_suffix_matrix(),
      cache_sb, *([cache_d] * n_seq))
    return o_sb.reshape(b, SB_WIDTH), o_d.reshape(b, DIFF_WIDTH)


def _suffix_matrix(n):
    j = jnp.arange(n)[:, None]
    s = jnp.arange(n)[None, :]
    return jnp.concatenate([(j > s), jnp.ones((n, n), bool)], axis=1).astype(BF16)


def _pick(m, candidates):
    for c in candidates:
        if m % c == 0:
            return c
    raise ValueError(f"no block size for {m}")


def kernel(x_prompt, x_sample, cache_sb_kv, cache_diff_kv, page_table, meta_tokens, w_in, lambda_q1, lambda_k1,
           lambda_q2, lambda_k2, diff_norm_g, w_branch_sb, w_branch_diff, w_out, ln_mix_g, ln_mix_b, w_up,
           w_down, ln_ffn_g, ln_ffn_b):
    assert w_in.shape[0] == DEPTH and x_prompt.shape[0] == 1 and x_sample.shape[1] == 1
    seq, d = x_prompt.shape[1:]
    n_dec = x_sample.shape[0]
    n_pool, page = cache_sb_kv.shape[1:3]
    past_len = page_table.shape[1] * page
    bq = _pick(seq, (256, 128))

    w_in_b = w_in[0].astype(BF16)
    w_sb_b, w_d_b, w_o_b = w_branch_sb[0].astype(BF16), w_branch_diff[0].astype(BF16), w_out[0].astype(BF16)
    w_up_b, w_down_b = w_up[0].astype(BF16), w_down[0].astype(BF16)
    lams = (lambda_q1, lambda_k1, lambda_q2, lambda_k2)

    xp = x_prompt[0]
    rope_p = _rope_table(N_META + jnp.arange(seq))
    qkv_p, sbkv_p, dkv_p, gates_p = _project(xp, w_in_b, rope_p, bm=_pick(seq, (1024, 512, 256, 128)), bn=512)
    x_small = jnp.concatenate([meta_tokens.astype(F32), x_sample[:, 0]], axis=0)
    pos_small = jnp.concatenate([jnp.arange(N_META), jnp.full((n_dec,), past_len)])
    qkv_s, sbkv_s, dkv_s, gates_s = _project(x_small, w_in_b, _rope_table(pos_small), bm=N_META + n_dec, bn=512)

    tri = _suffix_matrix(bq)
    qkv_meta = jnp.concatenate([qkv_s[:N_META], jnp.zeros((bq - N_META, QKV_COLS), BF16)], axis=0)
    o_sb = _sb_attention(qkv_p, qkv_meta, tri, bq=bq, n_heads=4)
    bqd = _pick(seq, (1024, 512, 256, 128))
    o_d = _diff_attention(qkv_p, qkv_meta, lams, diff_norm_g, bq=bqd, bk=bqd)
    x1 = _merge(xp, o_sb, o_d, gates_p, w_sb_b, w_d_b, w_o_b, ln_mix_g, ln_mix_b, bm=_pick(seq, (256, 128)))
    y_prompt = _ffn(x1, w_up_b, w_down_b, ln_ffn_g, ln_ffn_b, bm=_pick(seq, (512, 256, 128)), bf=1024)

    cache_sb = cache_sb_kv.reshape(n_pool * page, 2, N_SB_HEADS, HEAD_DIM)
    cache_d = cache_diff_kv.reshape(n_pool * page, 2, N_DIFF_HEADS, 2, HEAD_DIM).transpose(0, 1, 3, 2, 4).reshape(
        n_pool * page, 2, N_ROWS, HEAD_DIM)
    os_sb, os_d = _sample_attention(page_table, qkv_s[N_META:], cache_sb, cache_d, lams, diff_norm_g, page=page,
                                    n_seq=_pick(n_dec, (8, 4, 2, 1)))
    xs1 = _merge(x_sample[:, 0], os_sb, os_d, gates_s[N_META:], w_sb_b, w_d_b, w_o_b, ln_mix_g, ln_mix_b,
                 bm=n_dec)
    y_sample = _ffn(xs1, w_up_b, w_down_b, ln_ffn_g, ln_ffn_b, bm=n_dec, bf=512)

    t_p = N_META + seq
    sb_kv_prompt = jnp.concatenate([sbkv_s[:N_META], sbkv_p], axis=0).reshape(1, 1, t_p, 2, N_SB_HEADS, HEAD_DIM)
    diff_kv_prompt = jnp.concatenate([dkv_s[:N_META], dkv_p], axis=0).reshape(
        1, 1, t_p, 2, N_DIFF_HEADS, 2 * HEAD_DIM)
    sb_kv_sample = sbkv_s[N_META:].reshape(1, n_dec, 1, 2, N_SB_HEADS, HEAD_DIM)
    diff_kv_sample = dkv_s[N_META:].reshape(1, n_dec, 1, 2, N_DIFF_HEADS, 2 * HEAD_DIM)
    return (y_prompt[None], y_sample[:, None], sb_kv_prompt, diff_kv_prompt, sb_kv_sample, diff_kv_sample)
```

```python
import functools
import math

import jax
import jax.numpy as jnp
from jax import lax
from jax.experimental import pallas as pl
from jax.experimental.pallas import tpu as pltpu

F32 = jnp.float32
BF16 = jnp.bfloat16

HEAD_DIM = 128
N_SB_HEADS = 8
N_DIFF_HEADS = 4
SB_WIDTH = N_SB_HEADS * HEAD_DIM
DIFF_WIDTH = N_DIFF_HEADS * 2 * HEAD_DIM
N_META = 16
ROT_DIM = HEAD_DIM // 4
ROPE_THETA = 500000.0
LN_EPS = 1e-5
DEPTH = 1
ALPHA = (2 * DEPTH) ** 0.25
LAM_INIT = 0.8 - 0.6 * math.exp(-0.3 * 0)
QK_SCALE = HEAD_DIM ** -0.5
LANES = 128
VMEM_LIMIT = 56 * 1024 * 1024

COL_SBQ, COL_SBK, COL_SBV = 0, SB_WIDTH, 2 * SB_WIDTH
COL_DQ, COL_DK, COL_DV = 3 * SB_WIDTH, 3 * SB_WIDTH + DIFF_WIDTH, 3 * SB_WIDTH + 2 * DIFF_WIDTH
QKV_COLS = 3 * SB_WIDTH + 3 * DIFF_WIDTH
SEG = 1024


def _nt_dot(a, b):
    return lax.dot_general(a, b, (((1,), (1,)), ((), ())), preferred_element_type=F32)


def _dot(a, b):
    return jnp.dot(a, b, preferred_element_type=F32)


def _rope_table(pos):
    half = ROT_DIM // 2
    inv = ROPE_THETA ** (-jnp.arange(half, dtype=F32) / half)
    ang = pos.astype(F32)[:, None] * inv[None, :]
    cos, sin = jnp.cos(ang), jnp.sin(ang)
    t = pos.shape[0]
    rest1 = jnp.ones((t, HEAD_DIM - ROT_DIM), F32)
    rest0 = jnp.zeros((t, HEAD_DIM - ROT_DIM), F32)
    z = jnp.zeros((t, half), F32)
    c = jnp.concatenate([cos, cos, rest1], axis=1)
    s1 = jnp.concatenate([-sin, z, rest0], axis=1)
    s2 = jnp.concatenate([z, sin, rest0], axis=1)
    return jnp.concatenate([c, s1, s2], axis=1)


def _rope(z, rope):
    c, s1, s2 = rope[:, :LANES], rope[:, LANES:2 * LANES], rope[:, 2 * LANES:]
    half = ROT_DIM // 2
    outs = []
    for i in range(z.shape[1] // LANES):
        zc = z[:, i * LANES:(i + 1) * LANES]
        outs.append(zc * c + pltpu.roll(zc, LANES - half, 1) * s1 + pltpu.roll(zc, half, 1) * s2)
    return jnp.concatenate(outs, axis=1)


def _proj_kernel(x_ref, w_ref, rope_ref, qkv_ref, sbkv_ref, dkv_ref, gate_ref, xb_ref, *, nb):
    j = pl.program_id(1)

    @pl.when(j == 0)
    def _():
        xb_ref[...] = x_ref[...].astype(BF16)

    z = _dot(xb_ref[...], w_ref[...])
    seg = j // nb

    @pl.when(seg == 0)
    def _():
        qkv_ref[...] = (z * QK_SCALE).astype(BF16)

    @pl.when((seg == 1) | (seg == 2))
    def _():
        qkv_ref[...] = z.astype(BF16)
        sbkv_ref[...] = z

    @pl.when(seg == 3)
    def _():
        qkv_ref[...] = (_rope(z, rope_ref[...]) * QK_SCALE).astype(BF16)

    @pl.when(seg == 4)
    def _():
        r = _rope(z, rope_ref[...])
        qkv_ref[...] = r.astype(BF16)
        dkv_ref[...] = r

    @pl.when(seg == 5)
    def _():
        qkv_ref[...] = z.astype(BF16)
        dkv_ref[...] = z

    @pl.when(seg >= 6)
    def _():
        gate_ref[...] = (0.5 * jnp.tanh(0.5 * z) + 0.5).astype(BF16)


def _project(x, w_bf16, rope, *, bm, bn):
    m, d = x.shape
    n = w_bf16.shape[1]
    nb = SEG // bn
    grid = (m // bm, n // bn)
    return pl.pallas_call(
        functools.partial(_proj_kernel, nb=nb),
        grid=grid,
        in_specs=[
            pl.BlockSpec((bm, d), lambda i, j: (i, 0)),
            pl.BlockSpec((d, bn), lambda i, j: (0, j)),
            pl.BlockSpec((bm, 3 * LANES), lambda i, j: (i, 0)),
        ],
        out_specs=[
            pl.BlockSpec((bm, bn), lambda i, j: (i, jnp.minimum(j, 6 * nb - 1))),
            pl.BlockSpec((bm, bn), lambda i, j: (i, jnp.clip(j - nb, 0, 2 * nb - 1))),
            pl.BlockSpec((bm, bn), lambda i, j: (i, jnp.clip(j - 4 * nb, 0, 2 * nb - 1))),
            pl.BlockSpec((bm, bn), lambda i, j: (i, jnp.clip(j - 6 * nb, 0, 4 * nb - 1))),
        ],
        out_shape=[
            jax.ShapeDtypeStruct((m, QKV_COLS), BF16),
            jax.ShapeDtypeStruct((m, 2 * SB_WIDTH), F32),
            jax.ShapeDtypeStruct((m, 2 * DIFF_WIDTH), F32),
            jax.ShapeDtypeStruct((m, n - QKV_COLS), BF16),
        ],
        scratch_shapes=[pltpu.VMEM((bm, d), BF16)],
        compiler_params=pltpu.CompilerParams(
            dimension_semantics=("arbitrary", "arbitrary"), vmem_limit_bytes=VMEM_LIMIT),
        name="proj",
    )(x, w_bf16, rope)


def _split_dot(x, t):
    hi = x.astype(BF16)
    r1 = x - hi.astype(F32)
    mid = r1.astype(BF16)
    lo = (r1 - mid.astype(F32)).astype(BF16)
    return _dot(hi, t) + _dot(mid, t) + _dot(lo, t)


LOG_F32_ZERO = -104.0


def _sb_attn_kernel(q_ref, k_ref, v_ref, km_ref, vm_ref, tri_ref, o_ref, c_ref, acc_ref, *, bq, n_heads):
    qb = pl.program_id(1)
    tri = tri_ref[...]
    row = lax.broadcasted_iota(jnp.int32, (n_heads * bq, bq), 0) % bq
    col = lax.broadcasted_iota(jnp.int32, (n_heads * bq, bq), 1)
    head_cols = [slice(g * HEAD_DIM, (g + 1) * HEAD_DIM) for g in range(n_heads)]
    qs = [q_ref[:, hc] for hc in head_cols]

    def block(kv_rows, k_ref, v_ref, mask, c, acc):
        z = jnp.concatenate([_nt_dot(q, k_ref[kv_rows, hc]) for q, hc in zip(qs, head_cols)], axis=0)
        lk = -(jnp.maximum(z, 0.0) + jnp.log1p(jnp.exp(-jnp.abs(z))))
        if mask is not None:
            lk = jnp.where(mask, lk, 0.0)
        suffix = _split_dot(lk, tri)
        w = jnp.exp(z + lk + suffix + c)
        if mask is not None:
            w = jnp.where(mask, w, 0.0)
        w = w.astype(BF16)
        pv = [_dot(w[g * bq:(g + 1) * bq], v_ref[kv_rows, hc]) for g, hc in enumerate(head_cols)]
        block_total = suffix[:, :1] + lk[:, :1]
        return c + block_total, acc + jnp.concatenate(pv, axis=0)

    start = pl.multiple_of(qb * bq, bq)
    c, acc = block(pl.ds(start, bq), k_ref, v_ref, col < row,
                   jnp.zeros((n_heads * bq, bq), F32), jnp.zeros((n_heads * bq, HEAD_DIM), F32))
    c_ref[...] = c
    acc_ref[...] = acc

    def live(carry):
        kb, c_max = carry
        return (kb >= 0) & (c_max > LOG_F32_ZERO)

    def body(carry):
        kb, _ = carry
        s = pl.multiple_of(kb * bq, bq)
        c, acc = block(pl.ds(s, bq), k_ref, v_ref, None, c_ref[...], acc_ref[...])
        c_ref[...] = c
        acc_ref[...] = acc
        return kb - 1, jnp.max(c)

    _, c_max = lax.while_loop(live, body, (qb - 1, jnp.max(c)))

    @pl.when(c_max > LOG_F32_ZERO)
    def _():
        _, acc = block(slice(None), km_ref, vm_ref, col < N_META, c_ref[...], acc_ref[...])
        acc_ref[...] = acc

    for g, hc in enumerate(head_cols):
        o_ref[:, hc] = acc_ref[g * bq:(g + 1) * bq, :].astype(o_ref.dtype)


def _sb_attention(qkv, qkv_meta, tri, *, bq, n_heads):
    t = qkv.shape[0]
    w = n_heads * HEAD_DIM
    kb, vb = COL_SBK // w, COL_SBV // w
    return pl.pallas_call(
        functools.partial(_sb_attn_kernel, bq=bq, n_heads=n_heads),
        grid=(N_SB_HEADS // n_heads, t // bq),
        in_specs=[
            pl.BlockSpec((bq, w), lambda h, i: (i, h)),
            pl.BlockSpec((t, w), lambda h, i: (0, kb + h)),
            pl.BlockSpec((t, w), lambda h, i: (0, vb + h)),
            pl.BlockSpec((bq, w), lambda h, i: (0, kb + h)),
            pl.BlockSpec((bq, w), lambda h, i: (0, vb + h)),
            pl.BlockSpec((bq, bq), lambda h, i: (0, 0)),
        ],
        out_specs=pl.BlockSpec((bq, w), lambda h, i: (i, h)),
        out_shape=jax.ShapeDtypeStruct((t, SB_WIDTH), BF16),
        scratch_shapes=[pltpu.VMEM((n_heads * bq, bq), F32), pltpu.VMEM((n_heads * bq, HEAD_DIM), F32)],
        compiler_params=pltpu.CompilerParams(
            dimension_semantics=("arbitrary", "arbitrary"), vmem_limit_bytes=VMEM_LIMIT),
        name="sb_attn",
    )(qkv, qkv, qkv, qkv_meta, qkv_meta, tri)


def _lambda(lq1_ref, lk1_ref, lq2_ref, lk2_ref):
    a = jnp.sum(lq1_ref[...] * lk1_ref[...], axis=-1, keepdims=True)
    b = jnp.sum(lq2_ref[...] * lk2_ref[...], axis=-1, keepdims=True)
    return jnp.exp(a) - jnp.exp(b) + LAM_INIT


def _head_norm(o, g):
    of = o * lax.rsqrt(jnp.mean(jnp.square(o), axis=-1, keepdims=True) + LN_EPS)
    return (of * g) * (1.0 - LAM_INIT)


def _diff_attn_kernel(lq1_ref, lk1_ref, lq2_ref, lk2_ref, g_ref, q1_ref, q2_ref, k1_ref, k2_ref, v_ref,
                      km1_ref, km2_ref, vm_ref, o_ref, *, bq, bk):
    qb = pl.program_id(1)
    lam = _lambda(lq1_ref, lk1_ref, lq2_ref, lk2_ref)
    q1, q2 = q1_ref[...], q2_ref[...]
    n_meta_rows = km1_ref.shape[0]
    n_full = qb * (bq // bk)
    meta_mask = lax.broadcasted_iota(jnp.int32, (bq, n_meta_rows), 1) < N_META
    row = lax.broadcasted_iota(jnp.int32, (bq, bk), 0)
    col = lax.broadcasted_iota(jnp.int32, (bq, bk), 1)

    def scores(k1, k2, mask):
        s1, s2 = _nt_dot(q1, k1), _nt_dot(q2, k2)
        if mask is not None:
            s1, s2 = jnp.where(mask, s1, -jnp.inf), jnp.where(mask, s2, -jnp.inf)
        return s1, s2

    def lanewise(op, x):
        out = x[:, :LANES]
        for j in range(1, x.shape[1] // LANES):
            out = op(out, x[:, j * LANES:(j + 1) * LANES])
        return out

    def key_block(start):
        s = pl.multiple_of(start, bk)
        return k1_ref[pl.ds(s, bk), :], k2_ref[pl.ds(s, bk), :], v_ref[pl.ds(s, bk), :]

    def diag_blocks():
        for j in range(bq // bk):
            yield key_block(qb * bq + j * bk) + (col + j * bk <= row,)

    def max_update(mx, k1, k2, mask):
        s1, s2 = scores(k1, k2, mask)
        return jnp.maximum(mx[0], lanewise(jnp.maximum, s1)), jnp.maximum(mx[1], lanewise(jnp.maximum, s2))

    neg = jnp.full((bq, LANES), -jnp.inf, F32)
    mx = max_update((neg, neg), km1_ref[...], km2_ref[...], meta_mask)
    mx = lax.fori_loop(0, n_full, lambda i, mx: max_update(mx, *key_block(i * bk)[:2], None), mx)
    for k1, k2, _, mask in diag_blocks():
        mx = max_update(mx, k1, k2, mask)
    m1 = jnp.max(mx[0], axis=-1, keepdims=True)
    m2 = jnp.max(mx[1], axis=-1, keepdims=True)

    def acc_update(st, k1, k2, v, mask):
        s1, s2 = scores(k1, k2, mask)
        p1, p2 = jnp.exp(s1 - m1), jnp.exp(s2 - m2)
        return (st[0] + lanewise(jnp.add, p1), st[1] + _dot(p1.astype(BF16), v),
                st[2] + lanewise(jnp.add, p2), st[3] + _dot(p2.astype(BF16), v))

    zl, za = jnp.zeros((bq, LANES), F32), jnp.zeros((bq, 2 * HEAD_DIM), F32)
    st = acc_update((zl, za, zl, za), km1_ref[...], km2_ref[...], vm_ref[...], meta_mask)
    st = lax.fori_loop(0, n_full, lambda i, st: acc_update(st, *key_block(i * bk), None), st)
    for k1, k2, v, mask in diag_blocks():
        st = acc_update(st, k1, k2, v, mask)
    l1 = jnp.sum(st[0], axis=-1, keepdims=True)
    l2 = jnp.sum(st[2], axis=-1, keepdims=True)
    o = st[1] / l1 - lam * (st[3] / l2)
    o_ref[...] = _head_norm(o, g_ref[...]).astype(o_ref.dtype)


def _diff_attention(qkv, qkv_meta, lams, diff_norm_g, *, bq, bk):
    t = qkv.shape[0]
    mrows = qkv_meta.shape[0]
    qb, kb, vb = COL_DQ // HEAD_DIM, COL_DK // HEAD_DIM, COL_DV // (2 * HEAD_DIM)
    vec = pl.BlockSpec((1, HEAD_DIM), lambda h, i: (0, 0))
    return pl.pallas_call(
        functools.partial(_diff_attn_kernel, bq=bq, bk=bk),
        grid=(N_DIFF_HEADS, t // bq),
        in_specs=[
            vec, vec, vec, vec,
            pl.BlockSpec((1, 2 * HEAD_DIM), lambda h, i: (0, 0)),
            pl.BlockSpec((bq, HEAD_DIM), lambda h, i: (i, qb + 2 * h)),
            pl.BlockSpec((bq, HEAD_DIM), lambda h, i: (i, qb + 2 * h + 1)),
            pl.BlockSpec((t, HEAD_DIM), lambda h, i: (0, kb + 2 * h)),
            pl.BlockSpec((t, HEAD_DIM), lambda h, i: (0, kb + 2 * h + 1)),
            pl.BlockSpec((t, 2 * HEAD_DIM), lambda h, i: (0, vb + h)),
            pl.BlockSpec((mrows, HEAD_DIM), lambda h, i: (0, kb + 2 * h)),
            pl.BlockSpec((mrows, HEAD_DIM), lambda h, i: (0, kb + 2 * h + 1)),
            pl.BlockSpec((mrows, 2 * HEAD_DIM), lambda h, i: (0, vb + h)),
        ],
        out_specs=pl.BlockSpec((bq, 2 * HEAD_DIM), lambda h, i: (i, h)),
        out_shape=jax.ShapeDtypeStruct((t, DIFF_WIDTH), BF16),
        compiler_params=pltpu.CompilerParams(
            dimension_semantics=("arbitrary", "arbitrary"), vmem_limit_bytes=VMEM_LIMIT),
        name="diff_attn",
    )(*lams, diff_norm_g, qkv, qkv, qkv, qkv, qkv, qkv_meta, qkv_meta, qkv_meta)


def _layer_norm(x, g, b):
    mu = jnp.mean(x, axis=-1, keepdims=True)
    xc = x - mu
    var = jnp.mean(jnp.square(xc), axis=-1, keepdims=True)
    return (xc * lax.rsqrt(var + LN_EPS)) * g + b


def _merge_kernel(x_ref, osb_ref, od_ref, gsb_ref, gd_ref, wsb_ref, wd_ref, wo_ref, g_ref, b_ref, o_ref):
    br_sb = _dot(osb_ref[...], wsb_ref[...])
    br_d = _dot(od_ref[...], wd_ref[...])
    gated = gsb_ref[...].astype(F32) * br_sb + gd_ref[...].astype(F32) * br_d
    mix = _dot(gated.astype(BF16), wo_ref[...])
    o_ref[...] = _layer_norm(ALPHA * x_ref[...] + mix, g_ref[...], b_ref[...])


def _merge(x, o_sb, o_d, gates, w_sb, w_d, w_o, ln_g, ln_b, *, bm):
    m, d = x.shape
    row = lambda i: (i, 0)
    const = lambda i: (0, 0)
    return pl.pallas_call(
        _merge_kernel,
        grid=(m // bm,),
        in_specs=[
            pl.BlockSpec((bm, d), row),
            pl.BlockSpec((bm, SB_WIDTH), row),
            pl.BlockSpec((bm, DIFF_WIDTH), row),
            pl.BlockSpec((bm, d), row),
            pl.BlockSpec((bm, d), lambda i: (i, 1)),
            pl.BlockSpec((SB_WIDTH, d), const),
            pl.BlockSpec((DIFF_WIDTH, d), const),
            pl.BlockSpec((d, d), const),
            pl.BlockSpec((1, d), const),
            pl.BlockSpec((1, d), const),
        ],
        out_specs=pl.BlockSpec((bm, d), row),
        out_shape=jax.ShapeDtypeStruct((m, d), F32),
        compiler_params=pltpu.CompilerParams(
            dimension_semantics=("arbitrary",), vmem_limit_bytes=VMEM_LIMIT),
        name="merge",
    )(x, o_sb, o_d, gates, gates, w_sb, w_d, w_o, ln_g, ln_b)


def _ffn_kernel(x_ref, wu_ref, wd_ref, g_ref, b_ref, o_ref, xb_ref, acc_ref):
    f = pl.program_id(1)

    @pl.when(f == 0)
    def _():
        xb_ref[...] = x_ref[...].astype(BF16)
        acc_ref[...] = jnp.zeros_like(acc_ref)

    u = _dot(xb_ref[...], wu_ref[...])
    a = jnp.square(jnp.maximum(u, 0.0)).astype(BF16)
    acc_ref[...] += _dot(a, wd_ref[...])

    @pl.when(f == pl.num_programs(1) - 1)
    def _():
        o_ref[...] = _layer_norm(ALPHA * x_ref[...] + acc_ref[...], g_ref[...], b_ref[...])


def _ffn(x, w_up, w_down, ln_g, ln_b, *, bm, bf):
    m, d = x.shape
    dff = w_up.shape[1]
    return pl.pallas_call(
        _ffn_kernel,
        grid=(m // bm, dff // bf),
        in_specs=[
            pl.BlockSpec((bm, d), lambda i, f: (i, 0)),
            pl.BlockSpec((d, bf), lambda i, f: (0, f)),
            pl.BlockSpec((bf, d), lambda i, f: (f, 0)),
            pl.BlockSpec((1, d), lambda i, f: (0, 0)),
            pl.BlockSpec((1, d), lambda i, f: (0, 0)),
        ],
        out_specs=pl.BlockSpec((bm, d), lambda i, f: (i, 0)),
        out_shape=jax.ShapeDtypeStruct((m, d), F32),
        scratch_shapes=[pltpu.VMEM((bm, d), BF16), pltpu.VMEM((bm, d), F32)],
        compiler_params=pltpu.CompilerParams(
            dimension_semantics=("arbitrary", "arbitrary"), vmem_limit_bytes=VMEM_LIMIT),
        name="ffn",
    )(x, w_up, w_down, ln_g, ln_b)


N_ROWS = 8


def _sample_attn_kernel(pt_ref, lq1_ref, lk1_ref, lq2_ref, lk2_ref, g_ref, qsb_ref, qd_ref, kself_ref, vself_ref,
                        lt_ref, *refs, n_seq):
    sb_hbm, dc_refs = refs[0], refs[1:1 + n_seq]
    osb_ref, od_ref, c_ref, asb_ref, m_ref, l_ref, ad_ref, sb_buf, sb_sem, issued_ref = refs[1 + n_seq:]
    i, p = pl.program_id(0), pl.program_id(1)
    n_grp, n_pages = pl.num_programs(0), pl.num_programs(1)
    page = dc_refs[0].shape[0]
    n = page * N_ROWS
    n_col = n // LANES
    n_row = n_seq * N_ROWS
    half = N_ROWS // 2
    rows = lax.broadcasted_iota(jnp.int32, (n_row, n), 0) % N_ROWS
    lanes = lax.broadcasted_iota(jnp.int32, (n_row, n), 1)
    own = (lanes % N_ROWS) == rows
    slot = p % 2

    def page_matrix(ref, kv):
        return ref[:, kv].reshape(n, HEAD_DIM).astype(BF16)

    def sb_copy(grp, pg, slot_, u):
        start = pl.multiple_of(pt_ref[grp * n_seq + u, n_pages - 1 - pg] * page, page)
        return pltpu.make_async_copy(sb_hbm.at[pl.ds(start, page)], sb_buf.at[slot_, u], sb_sem.at[slot_, u])

    @pl.when(p == 0)
    def _():
        c_ref[...] = jnp.zeros_like(c_ref)
        asb_ref[...] = jnp.zeros_like(asb_ref)
        l_ref[...] = jnp.ones_like(l_ref)
        ad_ref[...] = vself_ref[...]
        qk = qd_ref[...].astype(F32) * kself_ref[...].astype(F32)
        s_self = jnp.sum(qk.reshape(n_row, HEAD_DIM), axis=-1, keepdims=True)
        m_ref[...] = jnp.broadcast_to(s_self, m_ref.shape)

    @pl.when((i == 0) & (p == 0))
    def _():
        for u in range(n_seq):
            sb_copy(0, 0, 0, u).start()
        issued_ref[0] = 1

    alive = jnp.max(c_ref[...]) > LOG_F32_ZERO
    last_page = p == n_pages - 1
    fetch_ahead = jnp.where(last_page, i + 1 < n_grp, p == 0)

    @pl.when(fetch_ahead)
    def _():
        grp = jnp.where(last_page, i + 1, i)
        pg = jnp.where(last_page, 0, p + 1)
        for u in range(n_seq):
            sb_copy(grp, pg, 1 - slot, u).start()

    issued_ref[1 - slot] = fetch_ahead.astype(jnp.int32)

    @pl.when(issued_ref[slot] == 1)
    def _():
        for u in range(n_seq):
            sb_copy(i, p, slot, u).wait()

    @pl.when(alive)
    def _():
        sbc_refs = [sb_buf.at[slot, u] for u in range(n_seq)]
        g = jnp.concatenate([_nt_dot(qsb_ref[u], page_matrix(sbc_refs[u], 0)) for u in range(n_seq)], axis=0)
        lk = jnp.where(own, -(jnp.maximum(g, 0.0) + jnp.log1p(jnp.exp(-jnp.abs(g)))), 0.0)
        lk_cols = jnp.concatenate([lk[:, j * LANES:(j + 1) * LANES] for j in range(n_col)], axis=0)
        s2 = _split_dot(lk_cols, lt_ref[...])
        c = c_ref[...]
        ws = [None] * n_col
        for j in reversed(range(n_col)):
            sl = slice(j * LANES, (j + 1) * LANES)
            e = g[:, sl] + lk[:, sl] + s2[j * n_row:(j + 1) * n_row, :LANES] + c
            ws[j] = jnp.where(own[:, sl], jnp.exp(e), 0.0)
            c = c + s2[j * n_row:(j + 1) * n_row, LANES:]
        c_ref[...] = jnp.where(own[:, :LANES], c, -jnp.inf)
        w = jnp.concatenate(ws, axis=1).astype(BF16)
        for u in range(n_seq):
            asb_ref[u] += _dot(w[u * N_ROWS:(u + 1) * N_ROWS], page_matrix(sbc_refs[u], 1))

    @pl.when((p > 0) & jnp.logical_not(last_page) & (jnp.max(c_ref[...]) > LOG_F32_ZERO))
    def _():
        for u in range(n_seq):
            sb_copy(i, p + 1, 1 - slot, u).start()
        issued_ref[1 - slot] = 1

    s = jnp.concatenate([_nt_dot(qd_ref[u], page_matrix(dc_refs[u], 0)) for u in range(n_seq)], axis=0)
    s = jnp.where(own, s, -jnp.inf)
    m = m_ref[...][:, :1]
    m_new = jnp.maximum(m, jnp.max(s, axis=-1, keepdims=True))
    a = jnp.exp(m - m_new)
    pr = jnp.exp(s - m_new)
    l_ref[...] = a * l_ref[...] + jnp.sum(pr, axis=-1, keepdims=True)
    m_ref[...] = jnp.broadcast_to(m_new, m_ref.shape)
    s_other = jnp.where(rows < half, pltpu.roll(s, half, 1), pltpu.roll(s, n - half, 1))
    pr_other = jnp.exp(s_other - m_new)
    for u in range(n_seq):
        sl = slice(u * N_ROWS, (u + 1) * N_ROWS)
        lhs = jnp.concatenate([pr[sl], pr_other[sl]], axis=0).astype(BF16)
        ad_ref[u] = jnp.concatenate([a[sl], a[sl]], axis=0) * ad_ref[u] + _dot(lhs, page_matrix(dc_refs[u], 1))

    @pl.when(p == pl.num_programs(1) - 1)
    def _():
        osb_ref[...] = asb_ref[...].astype(osb_ref.dtype)
        lam = _lambda(lq1_ref, lk1_ref, lq2_ref, lk2_ref)
        gain = g_ref[...]
        for u in range(n_seq):
            l = l_ref[u * N_ROWS:(u + 1) * N_ROWS, :1]
            x = ad_ref[u] / jnp.concatenate([l, l], axis=0)
            lo = x[0:half] - lam * x[3 * half:4 * half]
            hi = x[2 * half:3 * half] - lam * x[half:2 * half]
            ms = (jnp.sum(jnp.square(lo), axis=-1, keepdims=True)
                  + jnp.sum(jnp.square(hi), axis=-1, keepdims=True)) / (2 * HEAD_DIM)
            r = lax.rsqrt(ms + LN_EPS)
            od_ref[u] = jnp.concatenate([((lo * r) * gain[:, :HEAD_DIM]) * (1.0 - LAM_INIT),
                                         ((hi * r) * gain[:, HEAD_DIM:]) * (1.0 - LAM_INIT)],
                                        axis=1).astype(od_ref.dtype)


def _column_suffix_matrix():
    i = jnp.arange(LANES)
    same = (i[:, None] % N_ROWS) == (i[None, :] % N_ROWS)
    later = (i[:, None] // N_ROWS) > (i[None, :] // N_ROWS)
    return jnp.concatenate([same & later, same], axis=1).astype(BF16)


def _sample_attention(page_table, qkv_s, cache_sb, cache_d, lams, diff_norm_g, *, page, n_seq):
    b, n_pages = page_table.shape
    half = N_DIFF_HEADS

    def by_half(x):
        return x.reshape(b, N_DIFF_HEADS, 2, HEAD_DIM).transpose(0, 2, 1, 3).reshape(b, N_ROWS, HEAD_DIM)

    q_sb = qkv_s[:, COL_SBQ:COL_SBQ + SEG].reshape(b, N_ROWS, HEAD_DIM)
    q_d = by_half(qkv_s[:, COL_DQ:COL_DQ + SEG])
    k_self = by_half(qkv_s[:, COL_DK:COL_DK + SEG])
    v = by_half(qkv_s[:, COL_DV:COL_DV + SEG]).astype(F32)
    v_lo, v_hi = v[:, :half], v[:, half:]
    v_self = jnp.concatenate([v_lo, v_hi, v_hi, v_lo], axis=1)

    vec = pl.BlockSpec((1, HEAD_DIM), lambda i, p, pt: (0, 0))
    per_seq = lambda i, p, pt: (i, 0, 0)

    def page_spec(u):
        return pl.BlockSpec((page, 2, N_ROWS, HEAD_DIM),
                            lambda i, p, pt: (pt[i * n_seq + u, n_pages - 1 - p], 0, 0, 0))

    grid_spec = pltpu.PrefetchScalarGridSpec(
        num_scalar_prefetch=1,
        grid=(b // n_seq, n_pages),
        in_specs=[
            vec, vec, vec, vec,
            pl.BlockSpec((1, 2 * HEAD_DIM), lambda i, p, pt: (0, 0)),
            pl.BlockSpec((n_seq, N_ROWS, HEAD_DIM), per_seq),
            pl.BlockSpec((n_seq, N_ROWS, HEAD_DIM), per_seq),
            pl.BlockSpec((n_seq, N_ROWS, HEAD_DIM), per_seq),
            pl.BlockSpec((n_seq, 2 * N_ROWS, HEAD_DIM), per_seq),
            pl.BlockSpec((LANES, 2 * LANES), lambda i, p, pt: (0, 0)),
            pl.BlockSpec(memory_space=pl.ANY),
        ] + [page_spec(u) for u in range(n_seq)],
        out_specs=[
            pl.BlockSpec((n_seq, N_SB_HEADS, HEAD_DIM), per_seq),
            pl.BlockSpec((n_seq, N_DIFF_HEADS, 2 * HEAD_DIM), per_seq),
        ],
        scratch_shapes=[
            pltpu.VMEM((n_seq * N_ROWS, LANES), F32),
            pltpu.VMEM((n_seq, N_ROWS, HEAD_DIM), F32),
            pltpu.VMEM((n_seq * N_ROWS, LANES), F32),
            pltpu.VMEM((n_seq * N_ROWS, LANES), F32),
            pltpu.VMEM((n_seq, 2 * N_ROWS, HEAD_DIM), F32),
            pltpu.VMEM((2, n_seq, page, 2, N_ROWS, HEAD_DIM), F32),
            pltpu.SemaphoreType.DMA((2, n_seq)),
            pltpu.SMEM((2,), jnp.int32),
        ],
    )
    o_sb, o_d = pl.pallas_call(
        functools.partial(_sample_attn_kernel, n_seq=n_seq),
        grid_spec=grid_spec,
        out_shape=[
            jax.ShapeDtypeStruct((b, N_SB_HEADS, HEAD_DIM), BF16),
            jax.ShapeDtypeStruct((b, N_DIFF_HEADS, 2 * HEAD_DIM), BF16),
        ],
        compiler_params=pltpu.CompilerParams(
            dimension_semantics=("arbitrary", "arbitrary"), vmem_limit_bytes=VMEM_LIMIT),
        name="sample_attn",
    )(page_table, *lams, diff_norm_g, q_sb, q_d, k_self, v_self, _column_suffix_matrix(),
      cache_sb, *([cache_d] * n_seq))
    return o_sb.reshape(b, SB_WIDTH), o_d.reshape(b, DIFF_WIDTH)


def _suffix_matrix(n):
    j = jnp.arange(n)[:, None]
    s = jnp.arange(n)[None, :]
    return (j > s).astype(BF16)


def _pick(m, candidates):
    for c in candidates:
        if m % c == 0:
            return c
    raise ValueError(f"no block size for {m}")


def kernel(x_prompt, x_sample, cache_sb_kv, cache_diff_kv, page_table, meta_tokens, w_in, lambda_q1, lambda_k1,
           lambda_q2, lambda_k2, diff_norm_g, w_branch_sb, w_branch_diff, w_out, ln_mix_g, ln_mix_b, w_up,
           w_down, ln_ffn_g, ln_ffn_b):
    assert w_in.shape[0] == DEPTH and x_prompt.shape[0] == 1 and x_sample.shape[1] == 1
    seq, d = x_prompt.shape[1:]
    n_dec = x_sample.shape[0]
    n_pool, page = cache_sb_kv.shape[1:3]
    past_len = page_table.shape[1] * page
    bq = _pick(seq, (256, 128))

    w_in_b = w_in[0].astype(BF16)
    w_sb_b, w_d_b, w_o_b = w_branch_sb[0].astype(BF16), w_branch_diff[0].astype(BF16), w_out[0].astype(BF16)
    w_up_b, w_down_b = w_up[0].astype(BF16), w_down[0].astype(BF16)
    lams = (lambda_q1, lambda_k1, lambda_q2, lambda_k2)

    xp = x_prompt[0]
    rope_p = _rope_table(N_META + jnp.arange(seq))
    qkv_p, sbkv_p, dkv_p, gates_p = _project(xp, w_in_b, rope_p, bm=_pick(seq, (1024, 512, 256, 128)), bn=512)
    x_small = jnp.concatenate([meta_tokens.astype(F32), x_sample[:, 0]], axis=0)
    pos_small = jnp.concatenate([jnp.arange(N_META), jnp.full((n_dec,), past_len)])
    qkv_s, sbkv_s, dkv_s, gates_s = _project(x_small, w_in_b, _rope_table(pos_small), bm=N_META + n_dec, bn=512)

    tri = _suffix_matrix(bq)
    qkv_meta = jnp.concatenate([qkv_s[:N_META], jnp.zeros((bq - N_META, QKV_COLS), BF16)], axis=0)
    o_sb = _sb_attention(qkv_p, qkv_meta, tri, bq=bq, n_heads=4)
    bqd = _pick(seq, (1024, 512, 256, 128))
    o_d = _diff_attention(qkv_p, qkv_meta, lams, diff_norm_g, bq=bqd, bk=bqd)
    x1 = _merge(xp, o_sb, o_d, gates_p, w_sb_b, w_d_b, w_o_b, ln_mix_g, ln_mix_b, bm=_pick(seq, (256, 128)))
    y_prompt = _ffn(x1, w_up_b, w_down_b, ln_ffn_g, ln_ffn_b, bm=_pick(seq, (512, 256, 128)), bf=1024)

    cache_sb = cache_sb_kv.reshape(n_pool * page, 2, N_SB_HEADS, HEAD_DIM)
    cache_d = cache_diff_kv.reshape(n_pool * page, 2, N_DIFF_HEADS, 2, HEAD_DIM).transpose(0, 1, 3, 2, 4).reshape(
        n_pool * page, 2, N_ROWS, HEAD_DIM)
    os_sb, os_d = _sample_attention(page_table, qkv_s[N_META:], cache_sb, cache_d, lams, diff_norm_g, page=page,
                                    n_seq=_pick(n_dec, (8, 4, 2, 1)))
    xs1 = _merge(x_sample[:, 0], os_sb, os_d, gates_s[N_META:], w_sb_b, w_d_b, w_o_b, ln_mix_g, ln_mix_b,
                 bm=n_dec)
    y_sample = _ffn(xs1, w_up_b, w_down_b, ln_ffn_g, ln_ffn_b, bm=n_dec, bf=512)

    t_p = N_META + seq
    sb_kv_prompt = jnp.concatenate([sbkv_s[:N_META], sbkv_p], axis=0).reshape(1, 1, t_p, 2, N_SB_HEADS, HEAD_DIM)
    diff_kv_prompt = jnp.concatenate([dkv_s[:N_META], dkv_p], axis=0).reshape(
        1, 1, t_p, 2, N_DIFF_HEADS, 2 * HEAD_DIM)
    sb_kv_sample = sbkv_s[N_META:].reshape(1, n_dec, 1, 2, N_SB_HEADS, HEAD_DIM)
    diff_kv_sample = dkv_s[N_META:].reshape(1, n_dec, 1, 2, N_DIFF_HEADS, 2 * HEAD_DIM)
    return (y_prompt[None], y_sample[:, None], sb_kv_prompt, diff_kv_prompt, sb_kv_sample, diff_kv_sample)
```
